```python
import jax, jax.numpy as jnp
from jax import lax
import numpy as np

D_MODEL = 2048
BATCH = 4
SEQ = 2048
DEPTH = 4
DEC_BATCH = 128
DEC_SEQ = 1
PAST_LEN = 16384
PAGE_SIZE = 128

POOL_WIDTH = D_MODEL // 4
POOL_WINDOWS = (2, 4, 8, 16)
N_POOL_GROUPS = len(POOL_WINDOWS)
POOL_GROUP_DIM = POOL_WIDTH // N_POOL_GROUPS
POOL_BUF = max(POOL_WINDOWS) - 1
SC_WIDTH = D_MODEL // 4
SC_HEADS = 4
SC_KERNEL = 3
GDN_WIDTH = D_MODEL // 2
GDN_HEADS = 8
GDN_HEAD_DIM = GDN_WIDTH // GDN_HEADS
GDN_CONV = 4
GDN_CHUNK = 64
N_EXPERTS = 32
TOP_K = 4
D_FF = D_MODEL
SWIGLU_LIMIT = 7.0
SWIGLU_ALPHA = 1.702
EPS = 1e-6
IN_SIZES = (POOL_WIDTH, SC_WIDTH, SC_WIDTH, SC_WIDTH, GDN_WIDTH, GDN_WIDTH, GDN_WIDTH, GDN_WIDTH, GDN_HEADS, GDN_HEADS)
IN_COLS = sum(IN_SIZES)
IN_SPLITS = [int(s) for s in np.cumsum(IN_SIZES)[:-1]]

kernel_name = "hybrid_pool_shortconv_gdn_moe_adaln_step"


def rmsnorm(x, w):
    xf = x.astype(jnp.float32)
    y = xf * lax.rsqrt(jnp.mean(xf * xf, -1, keepdims=True) + EPS)
    return (y * w.astype(jnp.float32)).astype(x.dtype)


def l2norm(x):
    return x * lax.rsqrt(jnp.sum(x * x, -1, keepdims=True) + EPS)


def causal_dwconv(x, buf, w):
    K = w.shape[0]
    L = x.shape[1]
    xx = jnp.concatenate([buf.astype(x.dtype), x], axis=1)
    y = xx[:, 0:L] * w[0]
    for j in range(1, K):
        y = y + xx[:, j:j + L] * w[j]
    return y, xx[:, L:]


def pool_mixer(u, buf, start, w_grp, scale):
    Bsz, L, _ = u.shape
    uu = jnp.concatenate([buf.astype(u.dtype), u], axis=1).astype(jnp.float32)
    cs = jnp.concatenate([jnp.zeros((Bsz, 1, POOL_WIDTH), jnp.float32), jnp.cumsum(uu, axis=1)], axis=1)
    pos = start + jnp.arange(L)
    pooled = []
    for g, win in enumerate(POOL_WINDOWS):
        sl = slice(g * POOL_GROUP_DIM, (g + 1) * POOL_GROUP_DIM)
        wsum = cs[:, POOL_BUF + 1:POOL_BUF + 1 + L, sl] - cs[:, POOL_BUF + 1 - win:POOL_BUF + 1 - win + L, sl]
        cnt = jnp.minimum(pos + 1, win).astype(jnp.float32)
        pooled.append(wsum / cnt[None, :, None])
    d = jnp.concatenate(pooled, -1) - uu[:, POOL_BUF:]
    d = d.reshape(Bsz, L, N_POOL_GROUPS, POOL_GROUP_DIM)
    y = jnp.einsum('blgc,gcd->blgd', d, w_grp.astype(jnp.float32)).reshape(Bsz, L, POOL_WIDTH)
    y = y * scale.astype(jnp.float32)
    return y.astype(u.dtype), uu[:, L:].astype(u.dtype)


def short_conv_mixer(xs, bg, cg, buf, w_conv):
    y, new_buf = causal_dwconv(cg * xs, buf, w_conv)
    return bg * y, new_buf


def _delta_step(S, inp):
    qt, kt, vt, gt, bt = inp
    S = S * jnp.exp(gt)[..., None, None]
    kv = jnp.einsum('bhkv,bhk->bhv', S, kt)
    delta = (vt - kv) * bt[..., None]
    S = S + jnp.einsum('bhk,bhv->bhkv', kt, delta)
    o = jnp.einsum('bhkv,bhk->bhv', S, qt)
    return S, o


def gated_delta_rule(q, k, v, g, beta, S0):
    Bsz, L, H, Dk = q.shape
    q = l2norm(q) * (Dk ** -0.5)
    k = l2norm(k)
    if L % GDN_CHUNK:
        tm = lambda t: jnp.moveaxis(t, 1, 0)
        S, o = lax.scan(_delta_step, S0, (tm(q), tm(k), tm(v), tm(g), tm(beta)))
        return jnp.moveaxis(o, 0, 1), S
    C = GDN_CHUNK
    N = L // C

    def blk(t):
        t = jnp.moveaxis(t, 2, 1)
        return t.reshape((Bsz, H, N, C) + t.shape[3:])

    qc, kc, vc, bc = blk(q), blk(k), blk(v), blk(beta)
    gc = jnp.cumsum(blk(g), axis=-1)
    incl = jnp.tril(jnp.ones((C, C), bool))
    strict = jnp.tril(jnp.ones((C, C), bool), -1)
    decay = jnp.exp(jnp.where(incl, gc[..., :, None] - gc[..., None, :], -jnp.inf))
    kb = kc * bc[..., None]
    vb = vc * bc[..., None]
    Lmat = jnp.where(strict, jnp.einsum('bhncd,bhnsd->bhncs', kb, kc) * decay, 0.0)
    rhs = jnp.concatenate([vb, kb * jnp.exp(gc)[..., None]], axis=-1)
    sol = lax.linalg.triangular_solve(Lmat + jnp.eye(C, dtype=Lmat.dtype), rhs,
                                      left_side=True, lower=True, unit_diagonal=True)
    Dv = vc.shape[-1]
    u_val, k_cum = sol[..., :Dv], sol[..., Dv:]
    a_intra = jnp.where(incl, jnp.einsum('bhncd,bhnsd->bhncs', qc, kc) * decay, 0.0)

    def chunk_step(S, inp):
        qi, ki, ui, kci, ai, gi = inp
        v_new = ui - jnp.einsum('bhcd,bhdv->bhcv', kci, S)
        o = (jnp.einsum('bhcd,bhdv->bhcv', qi * jnp.exp(gi)[..., None], S)
             + jnp.einsum('bhcs,bhsv->bhcv', ai, v_new))
        glast = gi[..., -1]
        S = (S * jnp.exp(glast)[..., None, None]
             + jnp.einsum('bhcd,bhcv->bhdv', ki * jnp.exp(glast[..., None] - gi)[..., None], v_new))
        return S, o

    cm = lambda t: jnp.moveaxis(t, 2, 0)
    S, o = lax.scan(chunk_step, S0, (cm(qc), cm(kc), cm(u_val), cm(k_cum), cm(a_intra), cm(gc)))
    o = jnp.moveaxis(o, 0, 2).reshape(Bsz, H, L, Dv)
    return jnp.moveaxis(o, 1, 2), S


def gdn_mixer(q, k, v, z, a, b, conv_buf, S0, conv_w, a_log, dt_bias, onorm_w):
    Bsz, L, _ = q.shape
    qkv, conv_new = causal_dwconv(jnp.concatenate([q, k, v], -1), conv_buf, conv_w)
    qkv = jax.nn.silu(qkv).astype(jnp.float32)
    hs = (Bsz, L, GDN_HEADS, GDN_HEAD_DIM)
    qh, kh, vh = [t.reshape(hs) for t in jnp.split(qkv, 3, axis=-1)]
    g = -jnp.exp(a_log.astype(jnp.float32)) * jax.nn.softplus(a.astype(jnp.float32) + dt_bias.astype(jnp.float32))
    beta = jax.nn.sigmoid(b.astype(jnp.float32))
    o, S = gated_delta_rule(qh, kh, vh, g, beta, S0.astype(jnp.float32))
    o = o * lax.rsqrt(jnp.mean(o * o, -1, keepdims=True) + EPS) * onorm_w.astype(jnp.float32)
    o = o * jax.nn.silu(z.astype(jnp.float32).reshape(hs))
    return o.reshape(Bsz, L, GDN_WIDTH).astype(q.dtype), conv_new, S.astype(S0.dtype)


def moe(h, w_router, b_router, w_gu, b_gu, w_down, b_down):
    shp = h.shape
    t = h.reshape(-1, D_MODEL)
    logits = (t @ w_router + b_router).astype(jnp.float32)
    top_v, top_i = lax.top_k(logits, TOP_K)
    probs = jax.nn.softmax(top_v, axis=-1)
    gates = jnp.sum(jax.nn.one_hot(top_i, N_EXPERTS, dtype=jnp.float32) * probs[..., None], axis=1)
    acc = jnp.zeros(t.shape, jnp.float32)
    for e in range(N_EXPERTS):
        gu = t @ w_gu[e] + b_gu[e]
        gate = jnp.minimum(gu[:, :D_FF], SWIGLU_LIMIT)
        up = jnp.clip(gu[:, D_FF:], -SWIGLU_LIMIT, SWIGLU_LIMIT)
        act = (up + 1) * gate * jax.nn.sigmoid(SWIGLU_ALPHA * gate)
        acc = acc + gates[:, e:e + 1] * (act @ w_down[e] + b_down[e]).astype(jnp.float32)
    return acc.astype(h.dtype).reshape(shp)


def decoder_layer(x, c, pool_buf, sc_buf, qkv_buf, S0, start, p):
    (n1, n2, w_ada, b_ada, w_in, pool_w, pool_scale, sc_conv_w, qkv_conv_w, a_log, dt_bias,
     onorm_w, w_o, w_router, b_router, w_gu, b_gu, w_down, b_down) = p
    mod = jnp.einsum('bd,de->be', jax.nn.silu(c), w_ada) + b_ada
    sh1, sc1, g1, sh2, sc2, g2 = jnp.split(mod[:, None, :], 6, axis=-1)
    h = rmsnorm(x, n1) * (1 + sc1) + sh1
    proj = jnp.einsum('bld,de->ble', h, w_in)
    u_pool, x_sc, b_sc, c_sc, q, k, v, z, a, b = jnp.split(proj, IN_SPLITS, axis=-1)
    y_a, pool_new = pool_mixer(u_pool, pool_buf, start, pool_w, pool_scale)
    y_b, sc_new = short_conv_mixer(x_sc, b_sc, c_sc, sc_buf, sc_conv_w)
    y_c, qkv_new, S_new = gdn_mixer(q, k, v, z, a, b, qkv_buf, S0, qkv_conv_w, a_log, dt_bias, onorm_w)
    mixed = jnp.concatenate([y_a, y_b, y_c], axis=-1)
    x = x + g1 * jnp.einsum('ble,ed->bld', mixed, w_o)
    h = rmsnorm(x, n2) * (1 + sc2) + sh2
    x = x + g2 * moe(h, w_router, b_router, w_gu, b_gu, w_down, b_down)
    return x, pool_new, sc_new, qkv_new, S_new


def run_trunk(x, c, pool_buf, sc_buf, qkv_buf, delta_s, start, weights, final_norm):
    pools, scs, qkvs, deltas = [], [], [], []
    for l in range(DEPTH):
        p = tuple(w[l] for w in weights)
        x, pn, sn, qn, dn = decoder_layer(x, c, pool_buf[l], sc_buf[l], qkv_buf[l], delta_s[l], start, p)
        pools.append(pn); scs.append(sn); qkvs.append(qn); deltas.append(dn)
    return rmsnorm(x, final_norm), jnp.stack(pools), jnp.stack(scs), jnp.stack(qkvs), jnp.stack(deltas)


def setup_inputs(seed: int = 0) -> dict:
    key = jax.random.key(seed)
    ks = jax.random.split(key, 32)
    f32 = jnp.float32
    D = D_MODEL
    nrm = lambda k, shape, s: jax.random.normal(k, shape, f32) * s
    dt = jnp.exp(jax.random.uniform(ks[15], (DEPTH, GDN_HEADS), f32, np.log(1e-3), np.log(1e-1)))
    return {
        "x_prompt": nrm(ks[0], (BATCH, SEQ, D), 1.0),
        "x_sample": nrm(ks[1], (DEC_BATCH, DEC_SEQ, D), 1.0),
        "state_pool": nrm(ks[2], (DEPTH, DEC_BATCH, POOL_BUF, POOL_WIDTH), 1.0),
        "state_shortconv": nrm(ks[3], (DEPTH, DEC_BATCH, SC_KERNEL - 1, SC_WIDTH), 1.0),
        "state_qkv_conv": nrm(ks[4], (DEPTH, DEC_BATCH, GDN_CONV - 1, 3 * GDN_WIDTH), 1.0),
        "state_delta": nrm(ks[5], (DEPTH, DEC_BATCH, GDN_HEADS, GDN_HEAD_DIM, GDN_HEAD_DIM), 0.1),
        "c_prompt": nrm(ks[6], (BATCH, D), 1.0),
        "c_sample": nrm(ks[7], (DEC_BATCH, D), 1.0),
        "norm1": 1.0 + nrm(ks[8], (DEPTH, D), 0.02),
        "norm2": 1.0 + nrm(ks[9], (DEPTH, D), 0.02),
        "w_ada": nrm(ks[10], (DEPTH, D, 6 * D), 0.5 * D ** -0.5),
        "b_ada": nrm(ks[11], (DEPTH, 6 * D), 0.02),
        "w_in": nrm(ks[12], (DEPTH, D, IN_COLS), D ** -0.5),
        "pool_w": nrm(ks[13], (DEPTH, N_POOL_GROUPS, POOL_GROUP_DIM, POOL_GROUP_DIM), POOL_GROUP_DIM ** -0.5),
        "pool_scale": 1.0 + nrm(ks[14], (DEPTH, POOL_WIDTH), 0.02),
        "sc_conv_w": nrm(ks[16], (DEPTH, SC_KERNEL, SC_WIDTH), SC_KERNEL ** -0.5),
        "qkv_conv_w": nrm(ks[17], (DEPTH, GDN_CONV, 3 * GDN_WIDTH), GDN_CONV ** -0.5),
        "a_log": jnp.log(jax.random.uniform(ks[18], (DEPTH, GDN_HEADS), f32, 1.0, 16.0)),
        "dt_bias": dt + jnp.log(-jnp.expm1(-dt)),
        "onorm_w": 1.0 + nrm(ks[19], (DEPTH, GDN_HEAD_DIM), 0.02),
        "w_o": nrm(ks[20], (DEPTH, D, D), D ** -0.5),
        "w_router": nrm(ks[21], (DEPTH, D, N_EXPERTS), D ** -0.5),
        "b_router": nrm(ks[22], (DEPTH, N_EXPERTS), 0.01),
        "w_gate_up": nrm(ks[23], (DEPTH, N_EXPERTS, D, 2 * D_FF), D ** -0.5),
        "b_gate_up": nrm(ks[24], (DEPTH, N_EXPERTS, 2 * D_FF), 0.01),
        "w_down": nrm(ks[25], (DEPTH, N_EXPERTS, D_FF, D), D_FF ** -0.5),
        "b_down": nrm(ks[26], (DEPTH, N_EXPERTS, D), 0.01),
        "final_norm": 1.0 + nrm(ks[27], (D,), 0.02),
    }


def reference(x_prompt, x_sample, state_pool, state_shortconv, state_qkv_conv, state_delta,
              c_prompt, c_sample, norm1, norm2, w_ada, b_ada, w_in, pool_w, pool_scale,
              sc_conv_w, qkv_conv_w, a_log, dt_bias, onorm_w, w_o, w_router, b_router,
              w_gate_up, b_gate_up, w_down, b_down, final_norm):
    weights = (norm1, norm2, w_ada, b_ada, w_in, pool_w, pool_scale, sc_conv_w, qkv_conv_w, a_log,
               dt_bias, onorm_w, w_o, w_router, b_router, w_gate_up, b_gate_up, w_down, b_down)
    dt_p = x_prompt.dtype
    zero_pool = jnp.zeros((DEPTH, BATCH, POOL_BUF, POOL_WIDTH), dt_p)
    zero_sc = jnp.zeros((DEPTH, BATCH, SC_KERNEL - 1, SC_WIDTH), dt_p)
    zero_qkv = jnp.zeros((DEPTH, BATCH, GDN_CONV - 1, 3 * GDN_WIDTH), dt_p)
    zero_delta = jnp.zeros((DEPTH, BATCH, GDN_HEADS, GDN_HEAD_DIM, GDN_HEAD_DIM), dt_p)
    y_prompt, pool_p, sc_p, qkv_p, delta_p = run_trunk(
        x_prompt, c_prompt, zero_pool, zero_sc, zero_qkv, zero_delta, 0, weights, final_norm)
    y_sample, pool_s, sc_s, qkv_s, delta_s = run_trunk(
        x_sample, c_sample, state_pool, state_shortconv, state_qkv_conv, state_delta, PAST_LEN, weights, final_norm)
    return (y_prompt, y_sample, pool_p, pool_s, sc_p, sc_s, qkv_p, qkv_s, delta_p, delta_s)
```

```python
import functools

import jax
import jax.numpy as jnp
from jax import lax
from jax.experimental import pallas as pl
from jax.experimental.pallas import tpu as pltpu

F32 = jnp.float32
BF16 = jnp.bfloat16
I32 = jnp.int32
U32 = jnp.uint32

EPS = 1e-6
POOL_WINDOWS = (2, 4, 8, 16)
POOL_GROUP = 128
GDN_HEADS = 8
GDN_DIM = 128
GDN_CHUNK = 64
N_EXPERTS = 32
TOP_K = 4
SWIGLU_LIMIT = 7.0
SWIGLU_ALPHA = 1.702
LANES = 128
MOE_TM = 256
VMEM_LIMIT = 56 * 1024 * 1024


def _cp(sem):
    return pltpu.CompilerParams(dimension_semantics=sem, vmem_limit_bytes=VMEM_LIMIT)


def _silu(x):
    return x * jax.nn.sigmoid(x)


def _softplus(x):
    return jnp.maximum(x, 0.0) + jnp.log1p(jnp.exp(-jnp.abs(x)))


def _bdot(a, b):
    return jnp.dot(a.astype(BF16), b.astype(BF16), preferred_element_type=F32)


def _fdot(a, b):
    return jnp.dot(a, b, preferred_element_type=F32, precision=lax.Precision.HIGHEST)


def _shift_rows(x, s, row):
    return jnp.where(row >= s, pltpu.roll(x, s, 0), 0.0)


def _ada_kernel(c_ref, w_ref, b_ref, o_ref):
    o_ref[...] = _bdot(_silu(c_ref[...]), w_ref[...]) + b_ref[...]


def ada_mod(c_all, w_ada, b_ada, tn=1024):
    depth, d, n = w_ada.shape
    rows = c_all.shape[0]
    return pl.pallas_call(
        _ada_kernel,
        out_shape=jax.ShapeDtypeStruct((depth, rows, n), F32),
        grid=(depth, n // tn),
        in_specs=[
            pl.BlockSpec((rows, d), lambda l, j: (0, 0)),
            pl.BlockSpec((None, d, tn), lambda l, j: (l, 0, j)),
            pl.BlockSpec((None, 1, tn), lambda l, j: (l, 0, j)),
        ],
        out_specs=pl.BlockSpec((None, rows, tn), lambda l, j: (l, 0, j)),
        compiler_params=_cp(("arbitrary", "arbitrary")),
        name="ada_mod",
    )(c_all, w_ada, b_ada.reshape(depth, 1, n))


def _mod_rows(ref, per_row, brow):
    if per_row:
        return ref[...]
    return ref[pl.ds(brow, 1), :]


def _inproj_kernel(x_ref, sh_ref, sc_ref, n_ref, w_ref, wab_ref, o_ref, oab_ref, h_scr,
                   *, per_row, blocks_per_batch, n_ab):
    i = pl.program_id(0)
    j = pl.program_id(1)

    @pl.when(j == 0)
    def _():
        x = x_ref[...]
        y = x * lax.rsqrt(jnp.mean(x * x, -1, keepdims=True) + EPS) * n_ref[...]
        brow = i // blocks_per_batch
        h = y * (1.0 + _mod_rows(sc_ref, per_row, brow)) + _mod_rows(sh_ref, per_row, brow)
        hb = h.astype(BF16)
        h_scr[...] = hb
        col = lax.broadcasted_iota(I32, wab_ref.shape, 1)
        wab = jnp.where(col < n_ab, wab_ref[...], 0.0)
        oab_ref[...] = _bdot(hb, wab)

    o_ref[...] = jnp.dot(h_scr[...], w_ref[...].astype(BF16), preferred_element_type=F32)


def in_proj(x_all, mod, norm_w, w_in, layer, *, row0, rows, tm, tn, per_row, rows_per_batch, mod_row_blk):
    depth, d, n_cols = w_in.shape
    n_main = (n_cols // LANES) * LANES
    n_ab = n_cols - n_main
    mrows = tm if per_row else 8
    mod_spec = lambda chunk: pl.BlockSpec((None, mrows, d), lambda i, j: (layer, mod_row_blk, chunk))
    kern = functools.partial(_inproj_kernel, per_row=per_row,
                             blocks_per_batch=max(rows_per_batch // tm, 1), n_ab=n_ab)
    return pl.pallas_call(
        kern,
        out_shape=(jax.ShapeDtypeStruct((rows, n_main), F32), jax.ShapeDtypeStruct((rows, LANES), F32)),
        grid=(rows // tm, n_main // tn),
        in_specs=[
            pl.BlockSpec((tm, d), lambda i, j: (row0 // tm + i, 0)),
            mod_spec(0), mod_spec(1),
            pl.BlockSpec((None, 1, d), lambda i, j: (layer, 0, 0)),
            pl.BlockSpec((None, d, tn), lambda i, j: (layer, 0, j)),
            pl.BlockSpec((None, d, LANES), lambda i, j: (layer, 0, n_main // LANES)),
        ],
        out_specs=(pl.BlockSpec((tm, tn), lambda i, j: (i, j)),
                   pl.BlockSpec((tm, LANES), lambda i, j: (i, 0))),
        scratch_shapes=[pltpu.VMEM((tm, d), BF16)],
        compiler_params=_cp(("arbitrary", "arbitrary")),
        name="in_proj",
    )(x_all, mod, mod, norm_w.reshape(depth, 1, d), w_in, w_in)


def _pool_p_kernel(u_ref, w_ref, s_ref, o_ref):
    g = pl.program_id(1)
    u = u_ref[...]
    row = lax.broadcasted_iota(I32, u.shape, 0)
    w2 = u + _shift_rows(u, 1, row)
    w4 = w2 + _shift_rows(w2, 2, row)
    w8 = w4 + _shift_rows(w4, 4, row)
    w16 = w8 + _shift_rows(w8, 8, row)
    wsum = jnp.where(g == 0, w2, jnp.where(g == 1, w4, jnp.where(g == 2, w8, w16)))
    win = jnp.left_shift(2, g)
    cnt = jnp.minimum(row + 1, win).astype(F32)
    d = wsum / cnt - u
    o_ref[...] = (_bdot(d, w_ref[...]) * s_ref[...]).astype(o_ref.dtype)


def pool_prompt(proj, pool_w, pool_scale, layer, *, batch, seq):
    depth = pool_w.shape[0]
    ng = len(POOL_WINDOWS)
    return pl.pallas_call(
        _pool_p_kernel,
        out_shape=jax.ShapeDtypeStruct((batch * seq, ng * POOL_GROUP), BF16),
        grid=(batch, ng),
        in_specs=[
            pl.BlockSpec((seq, POOL_GROUP), lambda b, g: (b, g)),
            pl.BlockSpec((None, None, POOL_GROUP, POOL_GROUP), lambda b, g: (layer, g, 0, 0)),
            pl.BlockSpec((None, 1, POOL_GROUP), lambda b, g: (layer, 0, g)),
        ],
        out_specs=pl.BlockSpec((seq, POOL_GROUP), lambda b, g: (b, g)),
        compiler_params=_cp(("arbitrary", "arbitrary")),
        name="pool_prompt",
    )(proj, pool_w, pool_scale.reshape(depth, 1, ng * POOL_GROUP))


def _sc_p_kernel(x_ref, b_ref, c_ref, w_ref, o_ref, ns_ref):
    cx = c_ref[...] * x_ref[...]
    row = lax.broadcasted_iota(I32, cx.shape, 0)
    w = w_ref[...]
    y = _shift_rows(cx, 2, row) * w[0:1] + _shift_rows(cx, 1, row) * w[1:2] + cx * w[2:3]
    o_ref[...] = (b_ref[...] * y).astype(o_ref.dtype)
    n = cx.shape[0]
    ns_ref[...] = cx[n - 2:n, :]


def sc_prompt(proj, sc_conv_w, layer, *, batch, seq, width, col0, tc=256):
    nct = width // tc
    cb = col0 // tc
    return pl.pallas_call(
        _sc_p_kernel,
        out_shape=(jax.ShapeDtypeStruct((batch * seq, width), BF16),
                   jax.ShapeDtypeStruct((batch, 2, width), F32)),
        grid=(batch, nct),
        in_specs=[
            pl.BlockSpec((seq, tc), lambda b, c: (b, cb + c)),
            pl.BlockSpec((seq, tc), lambda b, c: (b, cb + nct + c)),
            pl.BlockSpec((seq, tc), lambda b, c: (b, cb + 2 * nct + c)),
            pl.BlockSpec((None, 3, tc), lambda b, c: (layer, 0, c)),
        ],
        out_specs=(pl.BlockSpec((seq, tc), lambda b, c: (b, c)),
                   pl.BlockSpec((None, 2, tc), lambda b, c: (b, 0, c))),
        compiler_params=_cp(("arbitrary", "arbitrary")),
        name="sc_prompt",
    )(proj, proj, proj, sc_conv_w)


def _inv_unit_lower(lm):
    c = lm.shape[0]
    r = lax.broadcasted_iota(I32, (c, c), 0)
    q = lax.broadcasted_iota(I32, (c, c), 1)
    p = -lm
    t = jnp.where(r == q, 1.0, 0.0) + p
    steps = max(c.bit_length() - 2, 0)
    for _ in range(steps):
        p = _fdot(p, p)
        t = t + _fdot(t, p)
    return t


def _gdn_p_kernel(qc_ref, kc_ref, vc_ref, qp_ref, kp_ref, vp_ref, z_ref, ab_ref,
                  cwq_ref, cwk_ref, cwv_ref, alog_ref, dtb_ref, on_ref,
                  o_ref, sout_ref, s_scr):
    n = pl.program_id(1)
    c = qc_ref.shape[0]
    hd = GDN_DIM

    @pl.when(n == 0)
    def _():
        s_scr[...] = jnp.zeros(s_scr.shape, F32)

    has_prev = n > 0

    def conv_silu(cur_ref, prev_ref, w_ref):
        prev = jnp.where(has_prev, prev_ref[...], 0.0)
        xx = jnp.concatenate([prev, cur_ref[...]], axis=0)
        w = w_ref[...]
        y = (xx[5:5 + c] * w[0:1] + xx[6:6 + c] * w[1:2]
             + xx[7:7 + c] * w[2:3] + xx[8:8 + c] * w[3:4])
        return _silu(y)

    q = conv_silu(qc_ref, qp_ref, cwq_ref)
    k = conv_silu(kc_ref, kp_ref, cwk_ref)
    v = conv_silu(vc_ref, vp_ref, cwv_ref)
    z = z_ref[...]

    ab = ab_ref[...]
    g_all = -jnp.exp(alog_ref[...]) * _softplus(ab + dtb_ref[...])
    beta_all = jax.nn.sigmoid(ab)
    row = lax.broadcasted_iota(I32, g_all.shape, 0)
    gc = g_all
    s = 1
    while s < c:
        gc = gc + _shift_rows(gc, s, row)
        s *= 2
    gct = gc.T
    eg_all = jnp.exp(gc)

    ri = lax.broadcasted_iota(I32, (c, c), 0)
    ci = lax.broadcasted_iota(I32, (c, c), 1)
    incl = ri >= ci
    strict = ri > ci
    on = on_ref[...]

    for h in range(GDN_HEADS):
        sl = slice(h * hd, (h + 1) * hd)
        qh, kh, vh = q[:, sl], k[:, sl], v[:, sl]
        qn = qh * lax.rsqrt(jnp.sum(qh * qh, -1, keepdims=True) + EPS) * (hd ** -0.5)
        kn = kh * lax.rsqrt(jnp.sum(kh * kh, -1, keepdims=True) + EPS)
        gcol = gc[:, h:h + 1]
        grow = gct[h:h + 1, :]
        egcol = eg_all[:, h:h + 1]
        bcol = beta_all[:, GDN_HEADS + h:GDN_HEADS + h + 1]
        decay = jnp.where(incl, jnp.exp(jnp.minimum(gcol - grow, 0.0)), 0.0)
        kb = kn * bcol
        vb = vh * bcol
        knb = kn.astype(BF16)
        nt = (((1,), (1,)), ((), ()))
        kk = lax.dot_general(kb.astype(BF16), knb, nt, preferred_element_type=F32)
        qk = lax.dot_general(qn.astype(BF16), knb, nt, preferred_element_type=F32)
        lm = jnp.where(strict, kk * decay, 0.0)
        a_intra = jnp.where(incl, qk * decay, 0.0)
        t = _inv_unit_lower(lm)
        u_val = _fdot(t, vb)
        k_cum = _fdot(t, kb * egcol)
        s_old = s_scr[h]
        v_new = u_val - _bdot(k_cum, s_old)
        o = _bdot(qn * egcol, s_old) + _bdot(a_intra, v_new)
        glast = gc[c - 1:c, h:h + 1]
        kd = kn * jnp.exp(glast - gcol)
        tn_dims = (((0,), (0,)), ((), ()))
        s_new = s_old * jnp.exp(glast) + lax.dot_general(
            kd.astype(BF16), v_new.astype(BF16), tn_dims, preferred_element_type=F32)
        s_scr[h] = s_new
        o = o * lax.rsqrt(jnp.mean(o * o, -1, keepdims=True) + EPS) * on
        o_ref[:, sl] = (o * _silu(z[:, sl])).astype(o_ref.dtype)

    @pl.when(n == pl.num_programs(1) - 1)
    def _():
        sout_ref[...] = s_scr[...]


def _head_row(v, offset):
    depth, h = v.shape
    return jnp.zeros((depth, 1, LANES), F32).at[:, 0, offset:offset + h].set(v.astype(F32))


def gdn_prompt(proj, ab, qkv_conv_w, alog_row, dtb_row, onorm_w, layer, *, batch, seq, col0):
    depth = qkv_conv_w.shape[0]
    wdt = GDN_HEADS * GDN_DIM
    c = GDN_CHUNK
    nchunk = seq // c
    cb = col0 // wdt
    cur = lambda t: pl.BlockSpec((c, wdt), lambda b, n: (b * nchunk + n, cb + t))
    prev = lambda t: pl.BlockSpec(
        (8, wdt), lambda b, n: (jnp.maximum((b * nchunk + n) * (c // 8) - 1, 0), cb + t))
    cw = lambda t: pl.BlockSpec((None, 4, wdt), lambda b, n: (layer, 0, t))
    row = pl.BlockSpec((None, 1, LANES), lambda b, n: (layer, 0, 0))
    return pl.pallas_call(
        _gdn_p_kernel,
        out_shape=(jax.ShapeDtypeStruct((batch * seq, wdt), BF16),
                   jax.ShapeDtypeStruct((batch, GDN_HEADS, GDN_DIM, GDN_DIM), F32)),
        grid=(batch, nchunk),
        in_specs=[cur(0), cur(1), cur(2), prev(0), prev(1), prev(2), cur(3),
                  pl.BlockSpec((c, LANES), lambda b, n: (b * nchunk + n, 0)),
                  cw(0), cw(1), cw(2), row, row, row],
        out_specs=(pl.BlockSpec((c, wdt), lambda b, n: (b * nchunk + n, 0)),
                   pl.BlockSpec((None, GDN_HEADS, GDN_DIM, GDN_DIM), lambda b, n: (b, 0, 0, 0))),
        scratch_shapes=[pltpu.VMEM((GDN_HEADS, GDN_DIM, GDN_DIM), F32)],
        compiler_params=_cp(("arbitrary", "arbitrary")),
        name="gdn_prompt",
    )(proj, proj, proj, proj, proj, proj, proj, ab,
      qkv_conv_w, qkv_conv_w, qkv_conv_w, alog_row, dtb_row,
      onorm_w.reshape(depth, 1, GDN_DIM))


def _mix_s_kernel(proj_ref, ab_ref, sp_ref, ssc_ref, sq_ref, pw_ref, ps_ref, scw_ref, cw_ref,
                  alog_ref, dtb_ref,
                  ya_ref, yb_ref, cx_ref, q_ref, k_ref, v_ref, eg_ref, beta_ref,
                  *, pool_w, sc_w, gdn_w):
    for g, win in enumerate(POOL_WINDOWS):
        sl = slice(g * POOL_GROUP, (g + 1) * POOL_GROUP)
        u = proj_ref[:, sl]
        acc = u
        nbuf = sp_ref.shape[0]
        for r in range(nbuf - (win - 1), nbuf):
            acc = acc + sp_ref[r, :, sl]
        d = acc / float(win) - u
        ya_ref[:, sl] = (_bdot(d, pw_ref[g]) * ps_ref[:, sl]).astype(ya_ref.dtype)
    c0 = pool_w
    xs = proj_ref[:, c0:c0 + sc_w]
    bg = proj_ref[:, c0 + sc_w:c0 + 2 * sc_w]
    cg = proj_ref[:, c0 + 2 * sc_w:c0 + 3 * sc_w]
    cx = cg * xs
    w = scw_ref[...]
    y = ssc_ref[0] * w[0:1] + ssc_ref[1] * w[1:2] + cx * w[2:3]
    yb_ref[...] = (bg * y).astype(yb_ref.dtype)
    cx_ref[...] = cx
    c1 = c0 + 3 * sc_w
    hd = GDN_DIM
    outs = (q_ref, k_ref, v_ref)
    for t in range(3):
        new = proj_ref[:, c1 + t * gdn_w:c1 + (t + 1) * gdn_w]
        ws = slice(t * gdn_w, (t + 1) * gdn_w)
        y = (sq_ref[0, :, ws] * cw_ref[0:1, ws] + sq_ref[1, :, ws] * cw_ref[1:2, ws]
             + sq_ref[2, :, ws] * cw_ref[2:3, ws] + new * cw_ref[3:4, ws])
        y = _silu(y)
        for h in range(GDN_HEADS):
            sl = slice(h * hd, (h + 1) * hd)
            yh = y[:, sl]
            if t == 0:
                yh = yh * lax.rsqrt(jnp.sum(yh * yh, -1, keepdims=True) + EPS) * (hd ** -0.5)
            elif t == 1:
                yh = yh * lax.rsqrt(jnp.sum(yh * yh, -1, keepdims=True) + EPS)
            outs[t][:, sl] = yh
    ab = ab_ref[...]
    eg_ref[...] = jnp.exp(-jnp.exp(alog_ref[...]) * _softplus(ab + dtb_ref[...]))
    beta_ref[...] = jax.nn.sigmoid(ab)


def mix_sample(proj, ab, sp_t, ssc_t, sq_t, pool_w, pool_scale, sc_conv_w, qkv_conv_w,
               alog_row, dtb_row, layer):
    nb = proj.shape[0]
    depth = pool_w.shape[0]
    pw = pool_scale.shape[1]
    scw = sc_conv_w.shape[2]
    gw = qkv_conv_w.shape[2] // 3
    full = lambda a: pl.BlockSpec(a.shape, lambda i: (0,) * a.ndim)
    lay = lambda a: pl.BlockSpec((None,) + a.shape[1:], lambda i: (layer,) + (0,) * (a.ndim - 1))
    ps3 = pool_scale.reshape(depth, 1, pw)
    kern = functools.partial(_mix_s_kernel, pool_w=pw, sc_w=scw, gdn_w=gw)
    shp = lambda w, dt: jax.ShapeDtypeStruct((nb, w), dt)
    outs = (shp(pw, BF16), shp(scw, BF16), shp(scw, F32), shp(gw, F32), shp(gw, F32), shp(gw, F32),
            shp(LANES, F32), shp(LANES, F32))
    return pl.pallas_call(
        kern,
        out_shape=outs,
        grid=(1,),
        in_specs=[full(proj), full(ab), full(sp_t), full(ssc_t), full(sq_t),
                  lay(pool_w), lay(ps3), lay(sc_conv_w), lay(qkv_conv_w), lay(alog_row), lay(dtb_row)],
        out_specs=tuple(pl.BlockSpec(o.shape, lambda i: (0, 0)) for o in outs),
        compiler_params=_cp(("arbitrary",)),
        name="mix_sample",
    )(proj, ab, sp_t, ssc_t, sq_t, pool_w, ps3, sc_conv_w, qkv_conv_w, alog_row, dtb_row)


def _delta_s_kernel(q_ref, k_ref, v_ref, eg_ref, beta_ref, z_ref, on_ref, s_ref, o_ref, so_ref, *, bb):
    i = pl.program_id(0)
    hd = GDN_DIM
    on = on_ref[...]
    tn_dims = (((0,), (0,)), ((), ()))

    def body(r, carry):
        t = i * bb + r
        qrow = q_ref[pl.ds(t, 1), :]
        krow = k_ref[pl.ds(t, 1), :]
        vrow = v_ref[pl.ds(t, 1), :]
        egrow = eg_ref[pl.ds(t, 1), :]
        brow = beta_ref[pl.ds(t, 1), :]
        zrow = z_ref[pl.ds(t, 1), :]
        for h in range(GDN_HEADS):
            sl = slice(h * hd, (h + 1) * hd)
            qh, kh, vh = qrow[:, sl], krow[:, sl], vrow[:, sl]
            eg = egrow[:, h:h + 1]
            bt = brow[:, GDN_HEADS + h:GDN_HEADS + h + 1]
            s_old = s_ref[r, h]
            kq = jnp.concatenate([kh, qh, jnp.zeros((6, hd), F32)], axis=0)
            res = _bdot(kq, s_old)
            delta = (vh - eg * res[0:1]) * bt
            qk = jnp.sum(qh * kh, -1, keepdims=True)
            o = eg * res[1:2] + qk * delta
            khi = kh.astype(BF16).astype(F32)
            dhi = delta.astype(BF16).astype(F32)
            zpad = jnp.zeros((13, hd), F32)
            lhs = jnp.concatenate([khi, kh - khi, khi, zpad], axis=0).astype(BF16)
            rhs = jnp.concatenate([dhi, dhi, delta - dhi, zpad], axis=0).astype(BF16)
            upd = lax.dot_general(lhs, rhs, tn_dims, preferred_element_type=F32)
            so_ref[r, h] = s_old * eg + upd
            o = o * lax.rsqrt(jnp.mean(o * o, -1, keepdims=True) + EPS) * on
            o_ref[r, :, sl] = o * _silu(zrow[:, sl])
        return carry

    lax.fori_loop(0, bb, body, 0)


def delta_sample(qn, kn, vv, eg, beta, proj, onorm_w, state_delta, layer, *, zcol_blk, bb=8):
    nb, gw = qn.shape
    depth = onorm_w.shape[0]
    full = lambda a: pl.BlockSpec(a.shape, lambda i: (0, 0))
    sblk = (None, bb, GDN_HEADS, GDN_DIM, GDN_DIM)
    o, s_new = pl.pallas_call(
        functools.partial(_delta_s_kernel, bb=bb),
        out_shape=(jax.ShapeDtypeStruct((nb, 1, gw), F32),
                   jax.ShapeDtypeStruct((nb, GDN_HEADS, GDN_DIM, GDN_DIM), F32)),
        grid=(nb // bb,),
        in_specs=[full(qn), full(kn), full(vv), full(eg), full(beta),
                  pl.BlockSpec((nb, gw), lambda i: (0, zcol_blk)),
                  pl.BlockSpec((None, 1, GDN_DIM), lambda i: (layer, 0, 0)),
                  pl.BlockSpec(sblk, lambda i: (layer, i, 0, 0, 0))],
        out_specs=(pl.BlockSpec((bb, 1, gw), lambda i: (i, 0, 0)),
                   pl.BlockSpec(sblk[1:], lambda i: (i, 0, 0, 0))),
        compiler_params=_cp(("arbitrary",)),
        name="delta_sample",
    )(qn, kn, vv, eg, beta, proj, onorm_w.reshape(depth, 1, GDN_DIM), state_delta)
    return o.reshape(nb, gw), s_new


def _pack_halves(h):
    half = h.shape[1] // 2
    lo = pltpu.bitcast(h[:, :half].astype(BF16).astype(F32), U32)
    hi = pltpu.bitcast(h[:, half:].astype(BF16).astype(F32), U32)
    return (lo >> 16) | (hi & jnp.uint32(0xFFFF0000))


def _unpack_halves(xu):
    lo = pltpu.bitcast(xu << 16, F32).astype(BF16)
    hi = pltpu.bitcast(xu & jnp.uint32(0xFFFF0000), F32).astype(BF16)
    return lo, hi


def _oproj_kernel(*refs, per_row, blocks_per_batch, n_alias):
    (ya_ref, yb_ref, yc_ref, x_ref, g1_ref, sh_ref, sc_ref, n_ref,
     woa_ref, wob_ref, woc_ref, wr_ref, br_ref) = refs[:13]
    x1_ref, xu_ref, ti_ref, tp_ref = refs[13 + n_alias:13 + n_alias + 4]
    i = pl.program_id(0)
    brow = i // blocks_per_batch
    acc = (jnp.dot(ya_ref[...].astype(BF16), woa_ref[...], preferred_element_type=F32)
           + jnp.dot(yb_ref[...].astype(BF16), wob_ref[...], preferred_element_type=F32)
           + jnp.dot(yc_ref[...].astype(BF16), woc_ref[...], preferred_element_type=F32))
    x1 = x_ref[...] + _mod_rows(g1_ref, per_row, brow) * acc
    x1_ref[...] = x1
    y = x1 * lax.rsqrt(jnp.mean(x1 * x1, -1, keepdims=True) + EPS) * n_ref[...]
    h = y * (1.0 + _mod_rows(sc_ref, per_row, brow)) + _mod_rows(sh_ref, per_row, brow)
    xu_ref[...] = _pack_halves(h)
    hhi = h.astype(BF16)
    hlo = (h - hhi.astype(F32)).astype(BF16)
    wr = wr_ref[...]
    whi = wr.astype(BF16)
    wlo = (wr - whi.astype(F32)).astype(BF16)
    logits = (jnp.dot(hhi, whi, preferred_element_type=F32)
              + jnp.dot(hlo, whi, preferred_element_type=F32)
              + jnp.dot(hhi, wlo, preferred_element_type=F32)) + br_ref[...]
    lane = lax.broadcasted_iota(I32, logits.shape, 1)
    neg = jnp.float32(-jnp.inf)
    cur = jnp.where(lane < N_EXPERTS, logits, neg)
    ti = jnp.zeros(logits.shape, I32)
    tv = jnp.full(logits.shape, neg, F32)
    for kk in range(TOP_K):
        m = jnp.max(cur, -1, keepdims=True)
        idx = jnp.min(jnp.where(cur == m, lane, LANES), -1, keepdims=True)
        ti = jnp.where(lane == kk, idx, ti)
        tv = jnp.where(lane == kk, m, tv)
        cur = jnp.where(lane == idx, neg, cur)
    e = jnp.exp(tv - jnp.max(tv, -1, keepdims=True))
    ti_ref[...] = ti
    tp_ref[...] = e / jnp.sum(e, -1, keepdims=True)


def o_proj(ya, yb, yc, x_all, mod, norm_w, w_o_bf, w_router_pad, b_router_pad, layer, *,
           row0, rows, tm, per_row, rows_per_batch, mod_row_blk, total_rows, prev=None):
    depth, d, _ = w_o_bf.shape
    wa, wb, wc = ya.shape[1], yb.shape[1], yc.shape[1]
    mrows = tm if per_row else 8
    rb = row0 // tm
    mod_spec = lambda chunk: pl.BlockSpec((None, mrows, d), lambda i: (layer, mod_row_blk, chunk))
    n_alias = 0 if prev is None else 4
    kern = functools.partial(_oproj_kernel, per_row=per_row,
                             blocks_per_batch=max(rows_per_batch // tm, 1), n_alias=n_alias)
    in_specs = [
        pl.BlockSpec((tm, wa), lambda i: (i, 0)),
        pl.BlockSpec((tm, wb), lambda i: (i, 0)),
        pl.BlockSpec((tm, wc), lambda i: (i, 0)),
        pl.BlockSpec((tm, d), lambda i: (rb + i, 0)),
        mod_spec(2), mod_spec(3), mod_spec(4),
        pl.BlockSpec((None, 1, d), lambda i: (layer, 0, 0)),
        pl.BlockSpec((None, wa, d), lambda i: (layer, 0, 0)),
        pl.BlockSpec((None, wb, d), lambda i: (layer, wa // wb, 0)),
        pl.BlockSpec((None, wc, d), lambda i: (layer, (wa + wb) // wc, 0)),
        pl.BlockSpec((None, d, LANES), lambda i: (layer, 0, 0)),
        pl.BlockSpec((None, 1, LANES), lambda i: (layer, 0, 0)),
    ]
    args = [ya, yb, yc, x_all, mod, mod, mod, norm_w.reshape(depth, 1, d),
            w_o_bf, w_o_bf, w_o_bf, w_router_pad, b_router_pad]
    aliases = {}
    if prev is not None:
        in_specs += [pl.BlockSpec(memory_space=pl.ANY)] * 4
        args += list(prev)
        aliases = {13 + t: t for t in range(4)}
    out_shape = (jax.ShapeDtypeStruct((total_rows, d), F32),
                 jax.ShapeDtypeStruct((total_rows, d // 2), U32),
                 jax.ShapeDtypeStruct((total_rows, LANES), I32),
                 jax.ShapeDtypeStruct((total_rows, LANES), F32))
    out_specs = (pl.BlockSpec((tm, d), lambda i: (rb + i, 0)),
                 pl.BlockSpec((tm, d // 2), lambda i: (rb + i, 0)),
                 pl.BlockSpec((tm, LANES), lambda i: (rb + i, 0)),
                 pl.BlockSpec((tm, LANES), lambda i: (rb + i, 0)))
    return pl.pallas_call(
        kern, out_shape=out_shape, grid=(rows // tm,),
        in_specs=in_specs, out_specs=out_specs,
        input_output_aliases=aliases,
        compiler_params=_cp(("arbitrary",)),
        name="o_proj",
    )(*args)


def _rank_kernel(ti_ref, rank_ref, cnt_ref, run_scr):
    i = pl.program_id(0)

    @pl.when(i == 0)
    def _():
        run_scr[...] = jnp.zeros(run_scr.shape, F32)

    ti = ti_ref[...]
    tb = ti.shape[0]
    lane = lax.broadcasted_iota(I32, ti.shape, 1)
    sel = jnp.zeros(ti.shape, F32)
    for kk in range(TOP_K):
        sel = sel + jnp.where(lane == ti[:, kk:kk + 1], 1.0, 0.0)
    r = lax.broadcasted_iota(I32, (tb, tb), 0)
    q = lax.broadcasted_iota(I32, (tb, tb), 1)
    tri = jnp.where(r > q, 1.0, 0.0)
    before = _bdot(tri, sel) + run_scr[0:1, :]
    rank = jnp.zeros(ti.shape, F32)
    for kk in range(TOP_K):
        rk = jnp.sum(jnp.where(lane == ti[:, kk:kk + 1], before, 0.0), -1, keepdims=True)
        rank = jnp.where(lane == kk, rk, rank)
    rank_ref[...] = rank.astype(I32)
    total = run_scr[0:1, :] + jnp.sum(sel, 0, keepdims=True)
    run_scr[...] = jnp.broadcast_to(total, run_scr.shape)
    cnt_ref[...] = jnp.broadcast_to(total, cnt_ref.shape).astype(I32)


def route_rank(topi, tb=128):
    t = topi.shape[0]
    return pl.pallas_call(
        _rank_kernel,
        out_shape=(jax.ShapeDtypeStruct((t, LANES), I32), jax.ShapeDtypeStruct((8, LANES), I32)),
        grid=(t // tb,),
        in_specs=[pl.BlockSpec((tb, LANES), lambda i: (i, 0))],
        out_specs=(pl.BlockSpec((tb, LANES), lambda i: (i, 0)),
                   pl.BlockSpec((8, LANES), lambda i: (0, 0))),
        scratch_shapes=[pltpu.VMEM((8, LANES), F32)],
        compiler_params=_cp(("arbitrary",)),
        name="route_rank",
    )(topi)


def _dispatch_kernel(pos_ref, xu_ref, xs_in_ref, xs_ref, sem, *, tb):
    del xs_in_ref
    i = pl.program_id(0)
    base = i * tb * TOP_K

    def row_copy(r, kk):
        p = pos_ref[base + r * TOP_K + kk]
        return pltpu.make_async_copy(xu_ref.at[pl.ds(r, 1)], xs_ref.at[pl.ds(p, 1)], sem)

    def start(r, c):
        for kk in range(TOP_K):
            row_copy(r, kk).start()
        return c

    def wait(r, c):
        for kk in range(TOP_K):
            row_copy(r, kk).wait()
        return c

    lax.fori_loop(0, tb, start, 0)
    lax.fori_loop(0, tb, wait, 0)


def dispatch(pos_flat, xu, xs_zero, tb=128):
    t, w = xu.shape
    return pl.pallas_call(
        functools.partial(_dispatch_kernel, tb=tb),
        out_shape=jax.ShapeDtypeStruct(xs_zero.shape, xs_zero.dtype),
        grid_spec=pltpu.PrefetchScalarGridSpec(
            num_scalar_prefetch=1,
            grid=(t // tb,),
            in_specs=[pl.BlockSpec((tb, w), lambda i, pos: (i, 0)),
                      pl.BlockSpec(memory_space=pl.ANY)],
            out_specs=pl.BlockSpec(memory_space=pl.ANY),
            scratch_shapes=[pltpu.SemaphoreType.DMA(())],
        ),
        input_output_aliases={2: 0},
        compiler_params=_cp(("arbitrary",)),
        name="moe_dispatch",
    )(pos_flat, xu, xs_zero)


def _gateup_kernel(we_ref, wj_ref, wi_ref, wv_ref, wf_ref,
                   x_ref, wg_ref, wu_ref, bg_ref, bu_ref, o_ref, wg_scr, wu_scr):
    w = pl.program_id(0)

    @pl.when(wf_ref[w] == 1)
    def _():
        wg_scr[...] = wg_ref[...].astype(BF16)
        wu_scr[...] = wu_ref[...].astype(BF16)

    @pl.when(wv_ref[w] == 1)
    def _():
        lo, hi = _unpack_halves(x_ref[...])
        half = lo.shape[1]
        gate = (jnp.dot(lo, wg_scr[:half], preferred_element_type=F32)
                + jnp.dot(hi, wg_scr[half:], preferred_element_type=F32) + bg_ref[...])
        up = (jnp.dot(lo, wu_scr[:half], preferred_element_type=F32)
              + jnp.dot(hi, wu_scr[half:], preferred_element_type=F32) + bu_ref[...])
        gate = jnp.minimum(gate, SWIGLU_LIMIT)
        up = jnp.clip(up, -SWIGLU_LIMIT, SWIGLU_LIMIT)
        o_ref[...] = ((up + 1.0) * gate * jax.nn.sigmoid(SWIGLU_ALPHA * gate)).astype(o_ref.dtype)


def moe_gate_up(work, xs, w_gate_up, b_gate_up, layer, *, tn):
    depth, ne, d, two_ff = w_gate_up.shape
    dff = two_ff // 2
    nj = dff // tn
    npad = xs.shape[0]
    nwork = work[0].shape[0]
    b4 = b_gate_up.reshape(depth, ne, 1, two_ff)
    return pl.pallas_call(
        _gateup_kernel,
        out_shape=jax.ShapeDtypeStruct((npad, dff), BF16),
        grid_spec=pltpu.PrefetchScalarGridSpec(
            num_scalar_prefetch=5,
            grid=(nwork,),
            in_specs=[
                pl.BlockSpec((MOE_TM, d // 2), lambda w, we, wj, wi, wv, wf: (wi[w], 0)),
                pl.BlockSpec((None, None, d, tn), lambda w, we, wj, wi, wv, wf: (layer, we[w], 0, wj[w])),
                pl.BlockSpec((None, None, d, tn), lambda w, we, wj, wi, wv, wf: (layer, we[w], 0, nj + wj[w])),
                pl.BlockSpec((None, None, 1, tn), lambda w, we, wj, wi, wv, wf: (layer, we[w], 0, wj[w])),
                pl.BlockSpec((None, None, 1, tn), lambda w, we, wj, wi, wv, wf: (layer, we[w], 0, nj + wj[w])),
            ],
            out_specs=pl.BlockSpec((MOE_TM, tn), lambda w, we, wj, wi, wv, wf: (wi[w], wj[w])),
            scratch_shapes=[pltpu.VMEM((d, tn), BF16), pltpu.VMEM((d, tn), BF16)],
        ),
        compiler_params=_cp(("arbitrary",)),
        name="moe_gate_up",
    )(*work, xs, w_gate_up, w_gate_up, b4, b4)


def _down_kernel(we_ref, wj_ref, wi_ref, wv_ref, wf_ref, a_ref, wd_ref, bd_ref, o_ref, wd_scr):
    w = pl.program_id(0)

    @pl.when(wf_ref[w] == 1)
    def _():
        wd_scr[...] = wd_ref[...].astype(BF16)

    @pl.when(wv_ref[w] == 1)
    def _():
        o_ref[...] = jnp.dot(a_ref[...], wd_scr[...], preferred_element_type=F32) + bd_ref[...]


def moe_down(work, act, w_down, b_down, layer, *, tn):
    depth, ne, dff, d = w_down.shape
    npad = act.shape[0]
    nwork = work[0].shape[0]
    b4 = b_down.reshape(depth, ne, 1, d)
    return pl.pallas_call(
        _down_kernel,
        out_shape=jax.ShapeDtypeStruct((npad, d), F32),
        grid_spec=pltpu.PrefetchScalarGridSpec(
            num_scalar_prefetch=5,
            grid=(nwork,),
            in_specs=[
                pl.BlockSpec((MOE_TM, dff), lambda w, we, wj, wi, wv, wf: (wi[w], 0)),
                pl.BlockSpec((None, None, dff, tn), lambda w, we, wj, wi, wv, wf: (layer, we[w], 0, wj[w])),
                pl.BlockSpec((None, None, 1, tn), lambda w, we, wj, wi, wv, wf: (layer, we[w], 0, wj[w])),
            ],
            out_specs=pl.BlockSpec((MOE_TM, tn), lambda w, we, wj, wi, wv, wf: (wi[w], wj[w])),
            scratch_shapes=[pltpu.VMEM((dff, tn), BF16)],
        ),
        compiler_params=_cp(("arbitrary",)),
        name="moe_down",
    )(*work, act, w_down, b4)


def make_work_list(counts, nj, nb_max):
    nblk = (counts + MOE_TM - 1) // MOE_TM
    blk_off = jnp.cumsum(nblk) - nblk
    total = jnp.sum(nblk)
    start = nj * blk_off
    nvalid = nj * total
    w = jnp.arange(nj * nb_max, dtype=I32)
    wc = jnp.minimum(w, nvalid - 1)
    e = jnp.clip(jnp.searchsorted(start, wc, side="right") - 1, 0, N_EXPERTS - 1).astype(I32)
    local = wc - start[e]
    ne = jnp.maximum(nblk[e], 1)
    j = local // ne
    il = local % ne
    valid = (w < nvalid).astype(I32)
    first = ((il == 0) & (w < nvalid)).astype(I32)
    return (e, j.astype(I32), (blk_off[e] + il).astype(I32), valid, first), blk_off * MOE_TM


def _combine_kernel(pos_ref, y_ref, tp_ref, x1_ref, g2p_ref, g2s_ref, o_ref, buf, sem,
                    *, tb, n_prompt_blocks, blocks_per_batch):
    i = pl.program_id(0)
    base = i * tb * TOP_K

    def row_copy(r, kk):
        p = pos_ref[base + r * TOP_K + kk]
        return pltpu.make_async_copy(y_ref.at[pl.ds(p, 1)], buf.at[kk, pl.ds(r, 1)], sem)

    def start(r, c):
        for kk in range(TOP_K):
            row_copy(r, kk).start()
        return c

    def wait(r, c):
        for kk in range(TOP_K):
            row_copy(r, kk).wait()
        return c

    lax.fori_loop(0, tb, start, 0)
    lax.fori_loop(0, tb, wait, 0)
    tp = tp_ref[...]
    moe = buf[0] * tp[:, 0:1]
    for kk in range(1, TOP_K):
        moe = moe + buf[kk] * tp[:, kk:kk + 1]
    brow = jnp.minimum(i // blocks_per_batch, 7)
    g2 = jnp.where(i >= n_prompt_blocks, g2s_ref[...], g2p_ref[pl.ds(brow, 1), :])
    o_ref[...] = x1_ref[...] + g2 * moe


def combine(pos_flat, y_sorted, topp, x1, mod, layer, *, n_prompt, rows_per_batch, prompt_mod_blk, tb=64):
    t, d = x1.shape
    npb = n_prompt // tb
    kern = functools.partial(_combine_kernel, tb=tb, n_prompt_blocks=npb,
                             blocks_per_batch=rows_per_batch // tb)
    return pl.pallas_call(
        kern,
        out_shape=jax.ShapeDtypeStruct((t, d), F32),
        grid_spec=pltpu.PrefetchScalarGridSpec(
            num_scalar_prefetch=1,
            grid=(t // tb,),
            in_specs=[
                pl.BlockSpec(memory_space=pl.ANY),
                pl.BlockSpec((tb, LANES), lambda i, pos: (i, 0)),
                pl.BlockSpec((tb, d), lambda i, pos: (i, 0)),
                pl.BlockSpec((None, 8, d), lambda i, pos: (layer, prompt_mod_blk, 5)),
                pl.BlockSpec((None, tb, d), lambda i, pos: (layer, jnp.maximum(i - npb, 0), 5)),
            ],
            out_specs=pl.BlockSpec((tb, d), lambda i, pos: (i, 0)),
            scratch_shapes=[pltpu.VMEM((TOP_K, tb, d), F32), pltpu.SemaphoreType.DMA(())],
        ),
        compiler_params=_cp(("arbitrary",)),
        name="moe_combine",
    )(pos_flat, y_sorted, topp, x1, mod, mod)


def _final_kernel(x_ref, w_ref, yp_ref, ys_ref, *, n_prompt_blocks):
    i = pl.program_id(0)
    x = x_ref[...]
    y = x * lax.rsqrt(jnp.mean(x * x, -1, keepdims=True) + EPS) * w_ref[...]

    @pl.when(i < n_prompt_blocks)
    def _():
        yp_ref[...] = y

    @pl.when(i >= n_prompt_blocks)
    def _():
        ys_ref[...] = y


def final_norm_split(x_all, w, *, n_prompt, tb):
    t, d = x_all.shape
    npb = n_prompt // tb
    return pl.pallas_call(
        functools.partial(_final_kernel, n_prompt_blocks=npb),
        out_shape=(jax.ShapeDtypeStruct((n_prompt, d), F32), jax.ShapeDtypeStruct((t - n_prompt, d), F32)),
        grid=(t // tb,),
        in_specs=[pl.BlockSpec((tb, d), lambda i: (i, 0)),
                  pl.BlockSpec((1, d), lambda i: (0, 0))],
        out_specs=(pl.BlockSpec((tb, d), lambda i: (jnp.minimum(i, npb - 1), 0)),
                   pl.BlockSpec((tb, d), lambda i: (jnp.maximum(i - npb, 0), 0))),
        compiler_params=_cp(("arbitrary",)),
        name="final_norm",
    )(x_all, w.reshape(1, d))


def kernel(x_prompt, x_sample, state_pool, state_shortconv, state_qkv_conv, state_delta,
           c_prompt, c_sample, norm1, norm2, w_ada, b_ada, w_in, pool_w, pool_scale,
           sc_conv_w, qkv_conv_w, a_log, dt_bias, onorm_w, w_o, w_router, b_router,
           w_gate_up, b_gate_up, w_down, b_down, final_norm):
    batch, seq, d = x_prompt.shape
    nb = x_sample.shape[0]
    depth = w_in.shape[0]
    n_prompt = batch * seq
    total = n_prompt + nb
    pool_width = pool_scale.shape[1]
    sc_width = sc_conv_w.shape[2]
    gdn_width = qkv_conv_w.shape[2] // 3
    col_sc = pool_width
    col_gdn = pool_width + 3 * sc_width
    tm_in = min(1024, seq)
    tm_o = min(512, seq)
    assert nb == LANES and seq % tm_in == 0 and seq % GDN_CHUNK == 0

    x_all = jnp.concatenate([x_prompt.reshape(n_prompt, d), x_sample.reshape(nb, d)], axis=0)
    c_all = jnp.concatenate([c_sample, c_prompt, jnp.zeros((8 - batch, d), F32)], axis=0)
    prompt_mod_blk = nb // 8
    mod = ada_mod(c_all, w_ada, b_ada)

    alog_row = _head_row(a_log, 0)
    dtb_row = _head_row(dt_bias, 0)
    w_o_bf = w_o.astype(BF16)
    wr_pad = jnp.zeros((depth, d, LANES), F32).at[:, :, :N_EXPERTS].set(w_router)
    br_pad = jnp.zeros((depth, 1, LANES), F32).at[:, 0, :N_EXPERTS].set(b_router)
    sp_t = jnp.swapaxes(state_pool, 1, 2)
    ssc_t = jnp.swapaxes(state_shortconv, 1, 2)
    sq_t = jnp.swapaxes(state_qkv_conv, 1, 2)

    nb_max = (total * TOP_K + N_EXPERTS * (MOE_TM - 1)) // MOE_TM
    npad = nb_max * MOE_TM
    tn_gu, tn_dn = 512, 512

    pools_p, pools_s, scs_p, scs_s, qkvs_p, qkvs_s, deltas_p, deltas_s = ([] for _ in range(8))
    for l in range(depth):
        proj_p, ab_p = in_proj(x_all, mod, norm1, w_in, l, row0=0, rows=n_prompt, tm=tm_in, tn=512,
                               per_row=False, rows_per_batch=seq, mod_row_blk=prompt_mod_blk)
        proj_s, ab_s = in_proj(x_all, mod, norm1, w_in, l, row0=n_prompt, rows=nb, tm=nb, tn=512,
                               per_row=True, rows_per_batch=nb, mod_row_blk=0)
        ya_p = pool_prompt(proj_p, pool_w, pool_scale, l, batch=batch, seq=seq)
        yb_p, sc_new_p = sc_prompt(proj_p, sc_conv_w, l, batch=batch, seq=seq, width=sc_width, col0=col_sc)
        yc_p, s_new_p = gdn_prompt(proj_p, ab_p, qkv_conv_w, alog_row, dtb_row, onorm_w, l,
                                   batch=batch, seq=seq, col0=col_gdn)
        ya_s, yb_s, cx_s, qn, kn, vv, eg, beta = mix_sample(
            proj_s, ab_s, sp_t[l], ssc_t[l], sq_t[l], pool_w, pool_scale, sc_conv_w, qkv_conv_w,
            alog_row, dtb_row, l)
        yc_s, s_new_s = delta_sample(qn, kn, vv, eg, beta, proj_s, onorm_w, state_delta, l,
                                     zcol_blk=(col_gdn + 3 * gdn_width) // gdn_width)

        p3 = proj_p.reshape(batch, seq, -1)
        pools_p.append(p3[:, seq - 15:, :pool_width])
        scs_p.append(sc_new_p)
        qkvs_p.append(p3[:, seq - 3:, col_gdn:col_gdn + 3 * gdn_width])
        deltas_p.append(s_new_p)
        pools_s.append(jnp.concatenate([state_pool[l][:, 1:], proj_s[:, None, :pool_width]], axis=1))
        scs_s.append(jnp.concatenate([state_shortconv[l][:, 1:], cx_s[:, None, :]], axis=1))
        qkvs_s.append(jnp.concatenate(
            [state_qkv_conv[l][:, 1:], proj_s[:, None, col_gdn:col_gdn + 3 * gdn_width]], axis=1))
        deltas_s.append(s_new_s)

        shared = o_proj(ya_p, yb_p, yc_p, x_all, mod, norm2, w_o_bf, wr_pad, br_pad, l,
                        row0=0, rows=n_prompt, tm=tm_o, per_row=False, rows_per_batch=seq,
                        mod_row_blk=prompt_mod_blk, total_rows=total)
        x1, xu, topi, topp = o_proj(ya_s, yb_s, yc_s, x_all, mod, norm2, w_o_bf, wr_pad, br_pad, l,
                                    row0=n_prompt, rows=nb, tm=nb, per_row=True, rows_per_batch=nb,
                                    mod_row_blk=0, total_rows=total, prev=shared)

        rank, cnt = route_rank(topi)
        counts = cnt[0, :N_EXPERTS]
        work_gu, row_off = make_work_list(counts, d // tn_gu, nb_max)
        work_dn, _ = make_work_list(counts, d // tn_dn, nb_max)
        ti4 = topi[:, :TOP_K]
        pos_flat = (row_off[ti4] + rank[:, :TOP_K]).reshape(-1).astype(I32)
        xs = dispatch(pos_flat, xu, jnp.zeros((npad, d // 2), U32))
        act = moe_gate_up(work_gu, xs, w_gate_up, b_gate_up, l, tn=tn_gu)
        y_sorted = moe_down(work_dn, act, w_down, b_down, l, tn=tn_dn)
        x_all = combine(pos_flat, y_sorted, topp, x1, mod, l, n_prompt=n_prompt,
                        rows_per_batch=seq, prompt_mod_blk=prompt_mod_blk)

    y_p, y_s = final_norm_split(x_all, final_norm, n_prompt=n_prompt, tb=nb)
    return (y_p.reshape(batch, seq, d), y_s.reshape(nb, 1, d),
            jnp.stack(pools_p), jnp.stack(pools_s), jnp.stack(scs_p), jnp.stack(scs_s),
            jnp.stack(qkvs_p), jnp.stack(qkvs_s), jnp.stack(deltas_p), jnp.stack(deltas_s))
```

```python
import functools

import jax
import jax.numpy as jnp
from jax import lax
from jax.experimental import pallas as pl
from jax.experimental.pallas import tpu as pltpu

F32 = jnp.float32
BF16 = jnp.bfloat16
I32 = jnp.int32
U32 = jnp.uint32

EPS = 1e-6
POOL_WINDOWS = (2, 4, 8, 16)
POOL_GROUP = 128
GDN_HEADS = 8
GDN_DIM = 128
GDN_CHUNK = 64
N_EXPERTS = 32
TOP_K = 4
SWIGLU_LIMIT = 7.0
SWIGLU_ALPHA = 1.702
LANES = 128
MOE_TM = 256
VMEM_LIMIT = 56 * 1024 * 1024


def _cp(sem):
    return pltpu.CompilerParams(dimension_semantics=sem, vmem_limit_bytes=VMEM_LIMIT)


def _silu(x):
    return x * jax.nn.sigmoid(x)


def _softplus(x):
    return jnp.maximum(x, 0.0) + jnp.log1p(jnp.exp(-jnp.abs(x)))


def _bdot(a, b):
    return jnp.dot(a.astype(BF16), b.astype(BF16), preferred_element_type=F32)


def _shift_rows(x, s, row):
    return jnp.where(row >= s, pltpu.roll(x, s, 0), 0.0)


def _ada_kernel(c_ref, w_ref, b_ref, o_ref):
    o_ref[...] = _bdot(_silu(c_ref[...]), w_ref[...]) + b_ref[...]


def ada_mod(c_all, w_ada, b_ada, tn=1024):
    depth, d, n = w_ada.shape
    rows = c_all.shape[0]
    return pl.pallas_call(
        _ada_kernel,
        out_shape=jax.ShapeDtypeStruct((depth, rows, n), F32),
        grid=(depth, n // tn),
        in_specs=[
            pl.BlockSpec((rows, d), lambda l, j: (0, 0)),
            pl.BlockSpec((None, d, tn), lambda l, j: (l, 0, j)),
            pl.BlockSpec((None, 1, tn), lambda l, j: (l, 0, j)),
        ],
        out_specs=pl.BlockSpec((None, rows, tn), lambda l, j: (l, 0, j)),
        compiler_params=_cp(("arbitrary", "arbitrary")),
        name="ada_mod",
    )(c_all, w_ada, b_ada.reshape(depth, 1, n))


def _mod_rows(ref, per_row, brow):
    if per_row:
        return ref[...]
    return ref[pl.ds(brow, 1), :]


def _inproj_kernel(x_ref, sh_ref, sc_ref, n_ref, w_ref, wab_ref, o_ref, oab_ref, h_scr,
                   *, per_row, blocks_per_batch, n_ab):
    i = pl.program_id(0)
    j = pl.program_id(1)

    @pl.when(j == 0)
    def _():
        x = x_ref[...]
        y = x * lax.rsqrt(jnp.mean(x * x, -1, keepdims=True) + EPS) * n_ref[...]
        brow = i // blocks_per_batch
        h = y * (1.0 + _mod_rows(sc_ref, per_row, brow)) + _mod_rows(sh_ref, per_row, brow)
        hb = h.astype(BF16)
        h_scr[...] = hb
        col = lax.broadcasted_iota(I32, wab_ref.shape, 1)
        wab = jnp.where(col < n_ab, wab_ref[...], 0.0)
        oab_ref[...] = _bdot(hb, wab)

    o_ref[...] = jnp.dot(h_scr[...], w_ref[...], preferred_element_type=F32)


def in_proj(x_all, mod, norm_w, w_in, layer, *, row0, rows, tm, tn, per_row, rows_per_batch, mod_row_blk):
    depth, d, n_cols = w_in.shape
    n_main = (n_cols // LANES) * LANES
    n_ab = n_cols - n_main
    mrows = tm if per_row else 8
    mod_spec = lambda chunk: pl.BlockSpec((None, mrows, d), lambda i, j: (layer, mod_row_blk, chunk))
    kern = functools.partial(_inproj_kernel, per_row=per_row,
                             blocks_per_batch=max(rows_per_batch // tm, 1), n_ab=n_ab)
    return pl.pallas_call(
        kern,
        out_shape=(jax.ShapeDtypeStruct((rows, n_main), F32), jax.ShapeDtypeStruct((rows, LANES), F32)),
        grid=(rows // tm, n_main // tn),
        in_specs=[
            pl.BlockSpec((tm, d), lambda i, j: (row0 // tm + i, 0)),
            mod_spec(0), mod_spec(1),
            pl.BlockSpec((None, 1, d), lambda i, j: (layer, 0, 0)),
            pl.BlockSpec((None, d, tn), lambda i, j: (layer, 0, j)),
            pl.BlockSpec((None, d, LANES), lambda i, j: (layer, 0, n_main // LANES)),
        ],
        out_specs=(pl.BlockSpec((tm, tn), lambda i, j: (i, j)),
                   pl.BlockSpec((tm, LANES), lambda i, j: (i, 0))),
        scratch_shapes=[pltpu.VMEM((tm, d), BF16)],
        compiler_params=_cp(("arbitrary", "arbitrary")),
        name="in_proj",
    )(x_all, mod, mod, norm_w.reshape(depth, 1, d), w_in, w_in)


def _pool_p_kernel(u_ref, w_ref, s_ref, o_ref):
    g = pl.program_id(1)
    u = u_ref[...]
    row = lax.broadcasted_iota(I32, u.shape, 0)
    w2 = u + _shift_rows(u, 1, row)
    w4 = w2 + _shift_rows(w2, 2, row)
    w8 = w4 + _shift_rows(w4, 4, row)
    w16 = w8 + _shift_rows(w8, 8, row)
    wsum = jnp.where(g == 0, w2, jnp.where(g == 1, w4, jnp.where(g == 2, w8, w16)))
    win = jnp.left_shift(2, g)
    cnt = jnp.minimum(row + 1, win).astype(F32)
    d = wsum / cnt - u
    o_ref[...] = (_bdot(d, w_ref[...]) * s_ref[...]).astype(o_ref.dtype)


def pool_prompt(proj, pool_w, pool_scale, layer, *, batch, seq):
    depth = pool_w.shape[0]
    ng = len(POOL_WINDOWS)
    return pl.pallas_call(
        _pool_p_kernel,
        out_shape=jax.ShapeDtypeStruct((batch * seq, ng * POOL_GROUP), BF16),
        grid=(batch, ng),
        in_specs=[
            pl.BlockSpec((seq, POOL_GROUP), lambda b, g: (b, g)),
            pl.BlockSpec((None, None, POOL_GROUP, POOL_GROUP), lambda b, g: (layer, g, 0, 0)),
            pl.BlockSpec((None, 1, POOL_GROUP), lambda b, g: (layer, 0, g)),
        ],
        out_specs=pl.BlockSpec((seq, POOL_GROUP), lambda b, g: (b, g)),
        compiler_params=_cp(("arbitrary", "arbitrary")),
        name="pool_prompt",
    )(proj, pool_w, pool_scale.reshape(depth, 1, ng * POOL_GROUP))


def _sc_p_kernel(x_ref, b_ref, c_ref, w_ref, o_ref, ns_ref):
    cx = c_ref[...] * x_ref[...]
    row = lax.broadcasted_iota(I32, cx.shape, 0)
    w = w_ref[...]
    y = _shift_rows(cx, 2, row) * w[0:1] + _shift_rows(cx, 1, row) * w[1:2] + cx * w[2:3]
    o_ref[...] = (b_ref[...] * y).astype(o_ref.dtype)
    n = cx.shape[0]
    ns_ref[...] = cx[n - 2:n, :]


def sc_prompt(proj, sc_conv_w, layer, *, batch, seq, width, col0, tc=256):
    nct = width // tc
    cb = col0 // tc
    return pl.pallas_call(
        _sc_p_kernel,
        out_shape=(jax.ShapeDtypeStruct((batch * seq, width), BF16),
                   jax.ShapeDtypeStruct((batch, 2, width), F32)),
        grid=(batch, nct),
        in_specs=[
            pl.BlockSpec((seq, tc), lambda b, c: (b, cb + c)),
            pl.BlockSpec((seq, tc), lambda b, c: (b, cb + nct + c)),
            pl.BlockSpec((seq, tc), lambda b, c: (b, cb + 2 * nct + c)),
            pl.BlockSpec((None, 3, tc), lambda b, c: (layer, 0, c)),
        ],
        out_specs=(pl.BlockSpec((seq, tc), lambda b, c: (b, c)),
                   pl.BlockSpec((None, 2, tc), lambda b, c: (b, 0, c))),
        compiler_params=_cp(("arbitrary", "arbitrary")),
        name="sc_prompt",
    )(proj, proj, proj, sc_conv_w)


def _inv_unit_lower_minus_eye(lms):
    c = lms[0].shape[0]
    ps = [-lm for lm in lms]
    rs = list(ps)
    steps = max(c.bit_length() - 2, 0)
    for _ in range(steps):
        ps = [_bdot(p, p) for p in ps]
        rs = [r + p + _bdot(r, p) for r, p in zip(rs, ps)]
    return rs


def _gdn_p_kernel(qc_ref, kc_ref, vc_ref, qp_ref, kp_ref, vp_ref, z_ref, ab_ref,
                  cwq_ref, cwk_ref, cwv_ref, alog_ref, dtb_ref, on_ref,
                  o_ref, sout_ref, s_scr):
    n = pl.program_id(1)
    c = qc_ref.shape[0]
    hd = GDN_DIM

    @pl.when(n == 0)
    def _():
        s_scr[...] = jnp.zeros(s_scr.shape, F32)

    has_prev = n > 0

    def conv_silu(cur_ref, prev_ref, w_ref):
        prev = jnp.where(has_prev, prev_ref[...], 0.0)
        xx = jnp.concatenate([prev, cur_ref[...]], axis=0)
        w = w_ref[...]
        y = (xx[5:5 + c] * w[0:1] + xx[6:6 + c] * w[1:2]
             + xx[7:7 + c] * w[2:3] + xx[8:8 + c] * w[3:4])
        return _silu(y)

    q = conv_silu(qc_ref, qp_ref, cwq_ref)
    k = conv_silu(kc_ref, kp_ref, cwk_ref)
    v = conv_silu(vc_ref, vp_ref, cwv_ref)
    z = z_ref[...]

    ab = ab_ref[...]
    g_all = -jnp.exp(alog_ref[...]) * _softplus(ab + dtb_ref[...])
    beta_all = jax.nn.sigmoid(ab)
    row = lax.broadcasted_iota(I32, g_all.shape, 0)
    gc = g_all
    s = 1
    while s < c:
        gc = gc + _shift_rows(gc, s, row)
        s *= 2
    gct = gc.T
    eg_all = jnp.exp(gc)

    ri = lax.broadcasted_iota(I32, (c, c), 0)
    ci = lax.broadcasted_iota(I32, (c, c), 1)
    incl = ri >= ci
    strict = ri > ci
    on = on_ref[...]

    heads = range(GDN_HEADS)
    nt = (((1,), (1,)), ((), ()))
    tn_dims = (((0,), (0,)), ((), ()))
    qns, kns, rhss, lms, aintras = [], [], [], [], []
    for h in heads:
        sl = slice(h * hd, (h + 1) * hd)
        qh, kh, vh = q[:, sl], k[:, sl], v[:, sl]
        qn = qh * lax.rsqrt(jnp.sum(qh * qh, -1, keepdims=True) + EPS) * (hd ** -0.5)
        kn = kh * lax.rsqrt(jnp.sum(kh * kh, -1, keepdims=True) + EPS)
        gcol = gc[:, h:h + 1]
        grow = gct[h:h + 1, :]
        bcol = beta_all[:, GDN_HEADS + h:GDN_HEADS + h + 1]
        decay = jnp.where(incl, jnp.exp(jnp.minimum(gcol - grow, 0.0)), 0.0)
        kb = kn * bcol
        knb = kn.astype(BF16)
        kk = lax.dot_general(kb.astype(BF16), knb, nt, preferred_element_type=F32)
        qk = lax.dot_general(qn.astype(BF16), knb, nt, preferred_element_type=F32)
        lms.append(jnp.where(strict, kk * decay, 0.0))
        aintras.append(jnp.where(incl, qk * decay, 0.0))
        rhss.append(jnp.concatenate([vh * bcol, kb * eg_all[:, h:h + 1]], axis=-1))
        qns.append(qn)
        kns.append(kn)
    rs = _inv_unit_lower_minus_eye(lms)
    sols = [rhs + _bdot(r, rhs) for r, rhs in zip(rs, rhss)]
    for h in heads:
        sl = slice(h * hd, (h + 1) * hd)
        gcol = gc[:, h:h + 1]
        u_val, k_cum = sols[h][:, :hd], sols[h][:, hd:]
        s_old = s_scr[h]
        v_new = u_val - _bdot(k_cum, s_old)
        o = _bdot(qns[h] * eg_all[:, h:h + 1], s_old) + _bdot(aintras[h], v_new)
        glast = gc[c - 1:c, h:h + 1]
        kd = kns[h] * jnp.exp(glast - gcol)
        s_scr[h] = s_old * jnp.exp(glast) + lax.dot_general(
            kd.astype(BF16), v_new.astype(BF16), tn_dims, preferred_element_type=F32)
        o = o * lax.rsqrt(jnp.mean(o * o, -1, keepdims=True) + EPS) * on
        o_ref[:, sl] = (o * _silu(z[:, sl])).astype(o_ref.dtype)

    @pl.when(n == pl.num_programs(1) - 1)
    def _():
        sout_ref[...] = s_scr[...]


def _head_row(v, offset):
    depth, h = v.shape
    return jnp.zeros((depth, 1, LANES), F32).at[:, 0, offset:offset + h].set(v.astype(F32))


def gdn_prompt(proj, ab, qkv_conv_w, alog_row, dtb_row, onorm_w, layer, *, batch, seq, col0):
    depth = qkv_conv_w.shape[0]
    wdt = GDN_HEADS * GDN_DIM
    c = GDN_CHUNK
    nchunk = seq // c
    cb = col0 // wdt
    cur = lambda t: pl.BlockSpec((c, wdt), lambda b, n: (b * nchunk + n, cb + t))
    prev = lambda t: pl.BlockSpec(
        (8, wdt), lambda b, n: (jnp.maximum((b * nchunk + n) * (c // 8) - 1, 0), cb + t))
    cw = lambda t: pl.BlockSpec((None, 4, wdt), lambda b, n: (layer, 0, t))
    row = pl.BlockSpec((None, 1, LANES), lambda b, n: (layer, 0, 0))
    return pl.pallas_call(
        _gdn_p_kernel,
        out_shape=(jax.ShapeDtypeStruct((batch * seq, wdt), BF16),
                   jax.ShapeDtypeStruct((batch, GDN_HEADS, GDN_DIM, GDN_DIM), F32)),
        grid=(batch, nchunk),
        in_specs=[cur(0), cur(1), cur(2), prev(0), prev(1), prev(2), cur(3),
                  pl.BlockSpec((c, LANES), lambda b, n: (b * nchunk + n, 0)),
                  cw(0), cw(1), cw(2), row, row, row],
        out_specs=(pl.BlockSpec((c, wdt), lambda b, n: (b * nchunk + n, 0)),
                   pl.BlockSpec((None, GDN_HEADS, GDN_DIM, GDN_DIM), lambda b, n: (b, 0, 0, 0))),
        scratch_shapes=[pltpu.VMEM((GDN_HEADS, GDN_DIM, GDN_DIM), F32)],
        compiler_params=_cp(("arbitrary", "arbitrary")),
        name="gdn_prompt",
    )(proj, proj, proj, proj, proj, proj, proj, ab,
      qkv_conv_w, qkv_conv_w, qkv_conv_w, alog_row, dtb_row,
      onorm_w.reshape(depth, 1, GDN_DIM))


def _mix_s_kernel(proj_ref, ab_ref, sp_ref, ssc_ref, sq_ref, pw_ref, ps_ref, scw_ref, cw_ref,
                  alog_ref, dtb_ref,
                  ya_ref, yb_ref, cx_ref, q_ref, k_ref, v_ref, eg_ref, beta_ref,
                  *, pool_w, sc_w, gdn_w):
    for g, win in enumerate(POOL_WINDOWS):
        sl = slice(g * POOL_GROUP, (g + 1) * POOL_GROUP)
        u = proj_ref[:, sl]
        acc = u
        nbuf = sp_ref.shape[0]
        for r in range(nbuf - (win - 1), nbuf):
            acc = acc + sp_ref[r, :, sl]
        d = acc / float(win) - u
        ya_ref[:, sl] = (_bdot(d, pw_ref[g]) * ps_ref[:, sl]).astype(ya_ref.dtype)
    c0 = pool_w
    xs = proj_ref[:, c0:c0 + sc_w]
    bg = proj_ref[:, c0 + sc_w:c0 + 2 * sc_w]
    cg = proj_ref[:, c0 + 2 * sc_w:c0 + 3 * sc_w]
    cx = cg * xs
    w = scw_ref[...]
    y = ssc_ref[0] * w[0:1] + ssc_ref[1] * w[1:2] + cx * w[2:3]
    yb_ref[...] = (bg * y).astype(yb_ref.dtype)
    cx_ref[...] = cx
    c1 = c0 + 3 * sc_w
    hd = GDN_DIM
    outs = (q_ref, k_ref, v_ref)
    for t in range(3):
        new = proj_ref[:, c1 + t * gdn_w:c1 + (t + 1) * gdn_w]
        ws = slice(t * gdn_w, (t + 1) * gdn_w)
        y = (sq_ref[0, :, ws] * cw_ref[0:1, ws] + sq_ref[1, :, ws] * cw_ref[1:2, ws]
             + sq_ref[2, :, ws] * cw_ref[2:3, ws] + new * cw_ref[3:4, ws])
        y = _silu(y)
        for h in range(GDN_HEADS):
            sl = slice(h * hd, (h + 1) * hd)
            yh = y[:, sl]
            if t == 0:
                yh = yh * lax.rsqrt(jnp.sum(yh * yh, -1, keepdims=True) + EPS) * (hd ** -0.5)
            elif t == 1:
                yh = yh * lax.rsqrt(jnp.sum(yh * yh, -1, keepdims=True) + EPS)
            outs[t][:, sl] = yh
    ab = ab_ref[...]
    eg_ref[...] = jnp.exp(-jnp.exp(alog_ref[...]) * _softplus(ab + dtb_ref[...]))
    beta_ref[...] = jax.nn.sigmoid(ab)


def mix_sample(proj, ab, sp_t, ssc_t, sq_t, pool_w, pool_scale, sc_conv_w, qkv_conv_w,
               alog_row, dtb_row, layer):
    nb = proj.shape[0]
    depth = pool_w.shape[0]
    pw = pool_scale.shape[1]
    scw = sc_conv_w.shape[2]
    gw = qkv_conv_w.shape[2] // 3
    full = lambda a: pl.BlockSpec(a.shape, lambda i: (0,) * a.ndim)
    lay = lambda a: pl.BlockSpec((None,) + a.shape[1:], lambda i: (layer,) + (0,) * (a.ndim - 1))
    ps3 = pool_scale.reshape(depth, 1, pw)
    kern = functools.partial(_mix_s_kernel, pool_w=pw, sc_w=scw, gdn_w=gw)
    shp = lambda w, dt: jax.ShapeDtypeStruct((nb, w), dt)
    outs = (shp(pw, BF16), shp(scw, BF16), shp(scw, F32), shp(gw, F32), shp(gw, F32), shp(gw, F32),
            shp(LANES, F32), shp(LANES, F32))
    return pl.pallas_call(
        kern,
        out_shape=outs,
        grid=(1,),
        in_specs=[full(proj), full(ab), full(sp_t), full(ssc_t), full(sq_t),
                  lay(pool_w), lay(ps3), lay(sc_conv_w), lay(qkv_conv_w), lay(alog_row), lay(dtb_row)],
        out_specs=tuple(pl.BlockSpec(o.shape, lambda i: (0, 0)) for o in outs),
        compiler_params=_cp(("arbitrary",)),
        name="mix_sample",
    )(proj, ab, sp_t, ssc_t, sq_t, pool_w, ps3, sc_conv_w, qkv_conv_w, alog_row, dtb_row)


def _delta_s_kernel(q_ref, k_ref, v_ref, eg_ref, beta_ref, z_ref, on_ref, s_ref, *rest, bb):
    o_ref, so_ref = rest[-2:]
    i = pl.program_id(0)
    hd = GDN_DIM
    on = on_ref[...]
    tn_dims = (((0,), (0,)), ((), ()))
    heads = range(GDN_HEADS)

    def body(r, carry):
        t = i * bb + r
        qrow = q_ref[pl.ds(t, 1), :]
        krow = k_ref[pl.ds(t, 1), :]
        vrow = v_ref[pl.ds(t, 1), :]
        egrow = eg_ref[pl.ds(t, 1), :]
        brow = beta_ref[pl.ds(t, 1), :]
        zrow = z_ref[pl.ds(t, 1), :]
        ress = []
        for h in heads:
            sl = slice(h * hd, (h + 1) * hd)
            kq = jnp.concatenate([krow[:, sl], qrow[:, sl], jnp.zeros((6, hd), F32)], axis=0)
            ress.append(_bdot(kq, s_ref[r, h]))
        upds = []
        for h in heads:
            sl = slice(h * hd, (h + 1) * hd)
            qh, kh, vh = qrow[:, sl], krow[:, sl], vrow[:, sl]
            eg = egrow[:, h:h + 1]
            bt = brow[:, GDN_HEADS + h:GDN_HEADS + h + 1]
            delta = (vh - eg * ress[h][0:1]) * bt
            qk = jnp.sum(qh * kh, -1, keepdims=True)
            o = eg * ress[h][1:2] + qk * delta
            o = o * lax.rsqrt(jnp.mean(o * o, -1, keepdims=True) + EPS) * on
            o_ref[r, :, sl] = o * _silu(zrow[:, sl])
            khi = kh.astype(BF16).astype(F32)
            dhi = delta.astype(BF16).astype(F32)
            zpad = jnp.zeros((13, hd), F32)
            lhs = jnp.concatenate([khi, kh - khi, khi, zpad], axis=0).astype(BF16)
            rhs = jnp.concatenate([dhi, dhi, delta - dhi, zpad], axis=0).astype(BF16)
            upds.append(lax.dot_general(lhs, rhs, tn_dims, preferred_element_type=F32))
        for h in heads:
            so_ref[r, h] = s_ref[r, h] * egrow[:, h:h + 1] + upds[h]
        return carry

    lax.fori_loop(0, bb, body, 0)


def delta_sample(qn, kn, vv, eg, beta, proj, onorm_w, state_delta, layer, *, zcol_blk, prev=None, bb=8):
    nb, gw = qn.shape
    depth = onorm_w.shape[0]
    full = lambda a: pl.BlockSpec(a.shape, lambda i: (0, 0))
    sblk = (None, bb, GDN_HEADS, GDN_DIM, GDN_DIM)
    in_specs = [full(qn), full(kn), full(vv), full(eg), full(beta),
                pl.BlockSpec((nb, gw), lambda i: (0, zcol_blk)),
                pl.BlockSpec((None, 1, GDN_DIM), lambda i: (layer, 0, 0)),
                pl.BlockSpec(sblk, lambda i: (layer, i, 0, 0, 0))]
    args = [qn, kn, vv, eg, beta, proj, onorm_w.reshape(depth, 1, GDN_DIM), state_delta]
    aliases = {}
    if prev is not None:
        in_specs.append(pl.BlockSpec(memory_space=pl.ANY))
        args.append(prev)
        aliases = {len(args) - 1: 1}
    o, s_new = pl.pallas_call(
        functools.partial(_delta_s_kernel, bb=bb),
        out_shape=(jax.ShapeDtypeStruct((nb, 1, gw), F32),
                   jax.ShapeDtypeStruct(state_delta.shape, F32)),
        grid=(nb // bb,),
        in_specs=in_specs,
        out_specs=(pl.BlockSpec((bb, 1, gw), lambda i: (i, 0, 0)),
                   pl.BlockSpec(sblk, lambda i: (layer, i, 0, 0, 0))),
        input_output_aliases=aliases,
        compiler_params=_cp(("arbitrary",)),
        name="delta_sample",
    )(*args)
    return o.reshape(nb, gw), s_new


def _pack_halves(h):
    half = h.shape[1] // 2
    lo = pltpu.bitcast(h[:, :half].astype(BF16).astype(F32), U32)
    hi = pltpu.bitcast(h[:, half:].astype(BF16).astype(F32), U32)
    return (lo >> 16) | (hi & jnp.uint32(0xFFFF0000))


def _unpack_halves(xu):
    lo = pltpu.bitcast(xu << 16, F32).astype(BF16)
    hi = pltpu.bitcast(xu & jnp.uint32(0xFFFF0000), F32).astype(BF16)
    return lo, hi


def _oproj_kernel(*refs, per_row, blocks_per_batch, n_alias):
    (ya_ref, yb_ref, yc_ref, x_ref, g1_ref, sh_ref, sc_ref, n_ref,
     woa_ref, wob_ref, woc_ref, wr_ref, br_ref) = refs[:13]
    x1_ref, xu_ref, ti_ref, tp_ref = refs[13 + n_alias:13 + n_alias + 4]
    i = pl.program_id(0)
    brow = i // blocks_per_batch
    acc = (jnp.dot(ya_ref[...].astype(BF16), woa_ref[...], preferred_element_type=F32)
           + jnp.dot(yb_ref[...].astype(BF16), wob_ref[...], preferred_element_type=F32)
           + jnp.dot(yc_ref[...].astype(BF16), woc_ref[...], preferred_element_type=F32))
    x1 = x_ref[...] + _mod_rows(g1_ref, per_row, brow) * acc
    x1_ref[...] = x1
    y = x1 * lax.rsqrt(jnp.mean(x1 * x1, -1, keepdims=True) + EPS) * n_ref[...]
    h = y * (1.0 + _mod_rows(sc_ref, per_row, brow)) + _mod_rows(sh_ref, per_row, brow)
    xu_ref[...] = _pack_halves(h)
    hhi = h.astype(BF16)
    hlo = (h - hhi.astype(F32)).astype(BF16)
    wr = wr_ref[...]
    whi = wr.astype(BF16)
    wlo = (wr - whi.astype(F32)).astype(BF16)
    logits = (jnp.dot(hhi, whi, preferred_element_type=F32)
              + jnp.dot(hlo, whi, preferred_element_type=F32)
              + jnp.dot(hhi, wlo, preferred_element_type=F32)) + br_ref[...]
    lane = lax.broadcasted_iota(I32, logits.shape, 1)
    neg = jnp.float32(-jnp.inf)
    cur = jnp.where(lane < N_EXPERTS, logits, neg)
    ti = jnp.zeros(logits.shape, I32)
    tv = jnp.full(logits.shape, neg, F32)
    for kk in range(TOP_K):
        m = jnp.max(cur, -1, keepdims=True)
        idx = jnp.min(jnp.where(cur == m, lane, LANES), -1, keepdims=True)
        ti = jnp.where(lane == kk, idx, ti)
        tv = jnp.where(lane == kk, m, tv)
        cur = jnp.where(lane == idx, neg, cur)
    e = jnp.exp(tv - jnp.max(tv, -1, keepdims=True))
    ti_ref[...] = ti
    tp_ref[...] = e / jnp.sum(e, -1, keepdims=True)


def o_proj(ya, yb, yc, x_all, mod, norm_w, w_o_bf, w_router_pad, b_router_pad, layer, *,
           row0, rows, tm, per_row, rows_per_batch, mod_row_blk, total_rows, prev=None):
    depth, d, _ = w_o_bf.shape
    wa, wb, wc = ya.shape[1], yb.shape[1], yc.shape[1]
    mrows = tm if per_row else 8
    rb = row0 // tm
    mod_spec = lambda chunk: pl.BlockSpec((None, mrows, d), lambda i: (layer, mod_row_blk, chunk))
    n_alias = 0 if prev is None else 4
    kern = functools.partial(_oproj_kernel, per_row=per_row,
                             blocks_per_batch=max(rows_per_batch // tm, 1), n_alias=n_alias)
    in_specs = [
        pl.BlockSpec((tm, wa), lambda i: (i, 0)),
        pl.BlockSpec((tm, wb), lambda i: (i, 0)),
        pl.BlockSpec((tm, wc), lambda i: (i, 0)),
        pl.BlockSpec((tm, d), lambda i: (rb + i, 0)),
        mod_spec(2), mod_spec(3), mod_spec(4),
        pl.BlockSpec((None, 1, d), lambda i: (layer, 0, 0)),
        pl.BlockSpec((None, wa, d), lambda i: (layer, 0, 0)),
        pl.BlockSpec((None, wb, d), lambda i: (layer, wa // wb, 0)),
        pl.BlockSpec((None, wc, d), lambda i: (layer, (wa + wb) // wc, 0)),
        pl.BlockSpec((None, d, LANES), lambda i: (layer, 0, 0)),
        pl.BlockSpec((None, 1, LANES), lambda i: (layer, 0, 0)),
    ]
    args = [ya, yb, yc, x_all, mod, mod, mod, norm_w.reshape(depth, 1, d),
            w_o_bf, w_o_bf, w_o_bf, w_router_pad, b_router_pad]
    aliases = {}
    if prev is not None:
        in_specs += [pl.BlockSpec(memory_space=pl.ANY)] * 4
        args += list(prev)
        aliases = {13 + t: t for t in range(4)}
    out_shape = (jax.ShapeDtypeStruct((total_rows, d), F32),
                 jax.ShapeDtypeStruct((total_rows, d // 2), U32),
                 jax.ShapeDtypeStruct((total_rows, LANES), I32),
                 jax.ShapeDtypeStruct((total_rows, LANES), F32))
    out_specs = (pl.BlockSpec((tm, d), lambda i: (rb + i, 0)),
                 pl.BlockSpec((tm, d // 2), lambda i: (rb + i, 0)),
                 pl.BlockSpec((tm, LANES), lambda i: (rb + i, 0)),
                 pl.BlockSpec((tm, LANES), lambda i: (rb + i, 0)))
    return pl.pallas_call(
        kern, out_shape=out_shape, grid=(rows // tm,),
        in_specs=in_specs, out_specs=out_specs,
        input_output_aliases=aliases,
        compiler_params=_cp(("arbitrary",)),
        name="o_proj",
    )(*args)


def _rank_kernel(ti_ref, rank_ref, cnt_ref, run_scr):
    i = pl.program_id(0)

    @pl.when(i == 0)
    def _():
        run_scr[...] = jnp.zeros(run_scr.shape, F32)

    ti = ti_ref[...]
    tb = ti.shape[0]
    lane = lax.broadcasted_iota(I32, ti.shape, 1)
    sel = jnp.zeros(ti.shape, F32)
    for kk in range(TOP_K):
        sel = sel + jnp.where(lane == ti[:, kk:kk + 1], 1.0, 0.0)
    r = lax.broadcasted_iota(I32, (tb, tb), 0)
    q = lax.broadcasted_iota(I32, (tb, tb), 1)
    tri = jnp.where(r > q, 1.0, 0.0)
    before = _bdot(tri, sel) + run_scr[0:1, :]
    rank = jnp.zeros(ti.shape, F32)
    for kk in range(TOP_K):
        rk = jnp.sum(jnp.where(lane == ti[:, kk:kk + 1], before, 0.0), -1, keepdims=True)
        rank = jnp.where(lane == kk, rk, rank)
    rank_ref[...] = rank.astype(I32)
    total = run_scr[0:1, :] + jnp.sum(sel, 0, keepdims=True)
    run_scr[...] = jnp.broadcast_to(total, run_scr.shape)
    cnt_ref[...] = jnp.broadcast_to(total, cnt_ref.shape).astype(I32)


def route_rank(topi, tb=128):
    t = topi.shape[0]
    return pl.pallas_call(
        _rank_kernel,
        out_shape=(jax.ShapeDtypeStruct((t, LANES), I32), jax.ShapeDtypeStruct((8, LANES), I32)),
        grid=(t // tb,),
        in_specs=[pl.BlockSpec((tb, LANES), lambda i: (i, 0))],
        out_specs=(pl.BlockSpec((tb, LANES), lambda i: (i, 0)),
                   pl.BlockSpec((8, LANES), lambda i: (0, 0))),
        scratch_shapes=[pltpu.VMEM((8, LANES), F32)],
        compiler_params=_cp(("arbitrary",)),
        name="route_rank",
    )(topi)


def _dispatch_kernel(pos_ref, xu_ref, xs_in_ref, xs_ref, sem, *, tb):
    del xs_in_ref
    i = pl.program_id(0)
    base = i * tb * TOP_K

    def row_copy(r, kk):
        p = pos_ref[base + r * TOP_K + kk]
        return pltpu.make_async_copy(xu_ref.at[pl.ds(r, 1)], xs_ref.at[pl.ds(p, 1)], sem)

    def start(r, c):
        for kk in range(TOP_K):
            row_copy(r, kk).start()
        return c

    def wait(r, c):
        for kk in range(TOP_K):
            row_copy(r, kk).wait()
        return c

    lax.fori_loop(0, tb, start, 0)
    lax.fori_loop(0, tb, wait, 0)


def dispatch(pos_flat, xu, xs_zero, tb=128):
    t, w = xu.shape
    return pl.pallas_call(
        functools.partial(_dispatch_kernel, tb=tb),
        out_shape=jax.ShapeDtypeStruct(xs_zero.shape, xs_zero.dtype),
        grid_spec=pltpu.PrefetchScalarGridSpec(
            num_scalar_prefetch=1,
            grid=(t // tb,),
            in_specs=[pl.BlockSpec((tb, w), lambda i, pos: (i, 0)),
                      pl.BlockSpec(memory_space=pl.ANY)],
            out_specs=pl.BlockSpec(memory_space=pl.ANY),
            scratch_shapes=[pltpu.SemaphoreType.DMA(())],
        ),
        input_output_aliases={2: 0},
        compiler_params=_cp(("arbitrary",)),
        name="moe_dispatch",
    )(pos_flat, xu, xs_zero)


def _gateup_kernel(we_ref, wj_ref, wi_ref, wv_ref, wf_ref,
                   x_ref, wg_ref, wu_ref, bg_ref, bu_ref, o_ref, wg_scr, wu_scr):
    w = pl.program_id(0)

    @pl.when(wf_ref[w] == 1)
    def _():
        wg_scr[...] = wg_ref[...].astype(BF16)
        wu_scr[...] = wu_ref[...].astype(BF16)

    @pl.when(wv_ref[w] == 1)
    def _():
        lo, hi = _unpack_halves(x_ref[...])
        half = lo.shape[1]
        gate = (jnp.dot(lo, wg_scr[:half], preferred_element_type=F32)
                + jnp.dot(hi, wg_scr[half:], preferred_element_type=F32) + bg_ref[...])
        up = (jnp.dot(lo, wu_scr[:half], preferred_element_type=F32)
              + jnp.dot(hi, wu_scr[half:], preferred_element_type=F32) + bu_ref[...])
        gate = jnp.minimum(gate, SWIGLU_LIMIT)
        up = jnp.clip(up, -SWIGLU_LIMIT, SWIGLU_LIMIT)
        o_ref[...] = ((up + 1.0) * gate * jax.nn.sigmoid(SWIGLU_ALPHA * gate)).astype(o_ref.dtype)


def moe_gate_up(work, xs, w_gate_up, b_gate_up, layer, *, tn):
    depth, ne, d, two_ff = w_gate_up.shape
    dff = two_ff // 2
    nj = dff // tn
    npad = xs.shape[0]
    nwork = work[0].shape[0]
    b4 = b_gate_up.reshape(depth, ne, 1, two_ff)
    return pl.pallas_call(
        _gateup_kernel,
        out_shape=jax.ShapeDtypeStruct((npad, dff), BF16),
        grid_spec=pltpu.PrefetchScalarGridSpec(
            num_scalar_prefetch=5,
            grid=(nwork,),
            in_specs=[
                pl.BlockSpec((MOE_TM, d // 2), lambda w, we, wj, wi, wv, wf: (wi[w], 0)),
                pl.BlockSpec((None, None, d, tn), lambda w, we, wj, wi, wv, wf: (layer, we[w], 0, wj[w])),
                pl.BlockSpec((None, None, d, tn), lambda w, we, wj, wi, wv, wf: (layer, we[w], 0, nj + wj[w])),
                pl.BlockSpec((None, None, 1, tn), lambda w, we, wj, wi, wv, wf: (layer, we[w], 0, wj[w])),
                pl.BlockSpec((None, None, 1, tn), lambda w, we, wj, wi, wv, wf: (layer, we[w], 0, nj + wj[w])),
            ],
            out_specs=pl.BlockSpec((MOE_TM, tn), lambda w, we, wj, wi, wv, wf: (wi[w], wj[w])),
            scratch_shapes=[pltpu.VMEM((d, tn), BF16), pltpu.VMEM((d, tn), BF16)],
        ),
        compiler_params=_cp(("arbitrary",)),
        name="moe_gate_up",
    )(*work, xs, w_gate_up, w_gate_up, b4, b4)


def _down_kernel(we_ref, wj_ref, wi_ref, wv_ref, wf_ref, a_ref, wd_ref, bd_ref, o_ref, wd_scr):
    w = pl.program_id(0)

    @pl.when(wf_ref[w] == 1)
    def _():
        wd_scr[...] = wd_ref[...].astype(BF16)

    @pl.when(wv_ref[w] == 1)
    def _():
        o_ref[...] = jnp.dot(a_ref[...], wd_scr[...], preferred_element_type=F32) + bd_ref[...]


def moe_down(work, act, w_down, b_down, layer, *, tn):
    depth, ne, dff, d = w_down.shape
    npad = act.shape[0]
    nwork = work[0].shape[0]
    b4 = b_down.reshape(depth, ne, 1, d)
    return pl.pallas_call(
        _down_kernel,
        out_shape=jax.ShapeDtypeStruct((npad, d), F32),
        grid_spec=pltpu.PrefetchScalarGridSpec(
            num_scalar_prefetch=5,
            grid=(nwork,),
            in_specs=[
                pl.BlockSpec((MOE_TM, dff), lambda w, we, wj, wi, wv, wf: (wi[w], 0)),
                pl.BlockSpec((None, None, dff, tn), lambda w, we, wj, wi, wv, wf: (layer, we[w], 0, wj[w])),
                pl.BlockSpec((None, None, 1, tn), lambda w, we, wj, wi, wv, wf: (layer, we[w], 0, wj[w])),
            ],
            out_specs=pl.BlockSpec((MOE_TM, tn), lambda w, we, wj, wi, wv, wf: (wi[w], wj[w])),
            scratch_shapes=[pltpu.VMEM((dff, tn), BF16)],
        ),
        compiler_params=_cp(("arbitrary",)),
        name="moe_down",
    )(*work, act, w_down, b4)


def make_work_list(counts, nj, nb_max):
    nblk = (counts + MOE_TM - 1) // MOE_TM
    blk_off = jnp.cumsum(nblk) - nblk
    total = jnp.sum(nblk)
    start = nj * blk_off
    nvalid = nj * total
    w = jnp.arange(nj * nb_max, dtype=I32)
    wc = jnp.minimum(w, nvalid - 1)
    e = jnp.clip(jnp.sum((start[None, :] <= wc[:, None]).astype(I32), axis=1) - 1, 0, N_EXPERTS - 1)
    local = wc - _table_lookup(start, e)
    ne = jnp.maximum(_table_lookup(nblk, e), 1)
    j = local // ne
    il = local % ne
    valid = (w < nvalid).astype(I32)
    first = ((il == 0) & (w < nvalid)).astype(I32)
    return (e, j.astype(I32), (_table_lookup(blk_off, e) + il).astype(I32), valid, first), blk_off * MOE_TM


def _table_lookup(table, idx):
    onehot = idx[..., None] == jnp.arange(table.shape[0], dtype=I32)
    return jnp.sum(jnp.where(onehot, table, 0), axis=-1).astype(I32)


def _combine_kernel(pos_ref, y_ref, tp_ref, x1_ref, g2p_ref, g2s_ref, o_ref, buf, sem,
                    *, tb, n_prompt_blocks, blocks_per_batch):
    i = pl.program_id(0)
    slot = i % 2

    def row_copy(blk, sl, r, kk):
        p = pos_ref[(blk * tb + r) * TOP_K + kk]
        return pltpu.make_async_copy(y_ref.at[pl.ds(p, 1)], buf.at[sl, kk, pl.ds(r, 1)], sem.at[sl])

    def start_block(blk, sl):
        def body(r, c):
            for kk in range(TOP_K):
                row_copy(blk, sl, r, kk).start()
            return c
        lax.fori_loop(0, tb, body, 0)

    def wait_block(blk, sl):
        def body(r, c):
            for kk in range(TOP_K):
                row_copy(blk, sl, r, kk).wait()
            return c
        lax.fori_loop(0, tb, body, 0)

    @pl.when(i == 0)
    def _():
        start_block(0, 0)

    @pl.when(i + 1 < pl.num_programs(0))
    def _():
        start_block(i + 1, 1 - slot)

    wait_block(i, slot)
    tp = tp_ref[...]
    moe = buf[slot, 0] * tp[:, 0:1]
    for kk in range(1, TOP_K):
        moe = moe + buf[slot, kk] * tp[:, kk:kk + 1]
    brow = jnp.minimum(i // blocks_per_batch, 7)
    g2 = jnp.where(i >= n_prompt_blocks, g2s_ref[...], g2p_ref[pl.ds(brow, 1), :])
    o_ref[...] = x1_ref[...] + g2 * moe


def combine(pos_flat, y_sorted, topp, x1, mod, layer, *, n_prompt, rows_per_batch, prompt_mod_blk, tb=64):
    t, d = x1.shape
    npb = n_prompt // tb
    kern = functools.partial(_combine_kernel, tb=tb, n_prompt_blocks=npb,
                             blocks_per_batch=rows_per_batch // tb)
    return pl.pallas_call(
        kern,
        out_shape=jax.ShapeDtypeStruct((t, d), F32),
        grid_spec=pltpu.PrefetchScalarGridSpec(
            num_scalar_prefetch=1,
            grid=(t // tb,),
            in_specs=[
                pl.BlockSpec(memory_space=pl.ANY),
                pl.BlockSpec((tb, LANES), lambda i, pos: (i, 0)),
                pl.BlockSpec((tb, d), lambda i, pos: (i, 0)),
                pl.BlockSpec((None, 8, d), lambda i, pos: (layer, prompt_mod_blk, 5)),
                pl.BlockSpec((None, tb, d), lambda i, pos: (layer, jnp.maximum(i - npb, 0), 5)),
            ],
            out_specs=pl.BlockSpec((tb, d), lambda i, pos: (i, 0)),
            scratch_shapes=[pltpu.VMEM((2, TOP_K, tb, d), F32), pltpu.SemaphoreType.DMA((2,))],
        ),
        compiler_params=_cp(("arbitrary",)),
        name="moe_combine",
    )(pos_flat, y_sorted, topp, x1, mod, mod)


def _final_kernel(x_ref, w_ref, yp_ref, ys_ref, *, n_prompt_blocks):
    i = pl.program_id(0)
    x = x_ref[...]
    y = x * lax.rsqrt(jnp.mean(x * x, -1, keepdims=True) + EPS) * w_ref[...]

    @pl.when(i < n_prompt_blocks)
    def _():
        yp_ref[...] = y

    @pl.when(i >= n_prompt_blocks)
    def _():
        ys_ref[...] = y


def final_norm_split(x_all, w, *, n_prompt, tb):
    t, d = x_all.shape
    npb = n_prompt // tb
    return pl.pallas_call(
        functools.partial(_final_kernel, n_prompt_blocks=npb),
        out_shape=(jax.ShapeDtypeStruct((n_prompt, d), F32), jax.ShapeDtypeStruct((t - n_prompt, d), F32)),
        grid=(t // tb,),
        in_specs=[pl.BlockSpec((tb, d), lambda i: (i, 0)),
                  pl.BlockSpec((1, d), lambda i: (0, 0))],
        out_specs=(pl.BlockSpec((tb, d), lambda i: (jnp.minimum(i, npb - 1), 0)),
                   pl.BlockSpec((tb, d), lambda i: (jnp.maximum(i - npb, 0), 0))),
        compiler_params=_cp(("arbitrary",)),
        name="final_norm",
    )(x_all, w.reshape(1, d))


def kernel(x_prompt, x_sample, state_pool, state_shortconv, state_qkv_conv, state_delta,
           c_prompt, c_sample, norm1, norm2, w_ada, b_ada, w_in, pool_w, pool_scale,
           sc_conv_w, qkv_conv_w, a_log, dt_bias, onorm_w, w_o, w_router, b_router,
           w_gate_up, b_gate_up, w_down, b_down, final_norm):
    batch, seq, d = x_prompt.shape
    nb = x_sample.shape[0]
    depth = w_in.shape[0]
    n_prompt = batch * seq
    total = n_prompt + nb
    pool_width = pool_scale.shape[1]
    sc_width = sc_conv_w.shape[2]
    gdn_width = qkv_conv_w.shape[2] // 3
    col_sc = pool_width
    col_gdn = pool_width + 3 * sc_width
    tm_in = min(1024, seq)
    tm_o = min(512, seq)
    assert nb == LANES and seq % tm_in == 0 and seq % GDN_CHUNK == 0

    x_all = jnp.concatenate([x_prompt.reshape(n_prompt, d), x_sample.reshape(nb, d)], axis=0)
    c_all = jnp.concatenate([c_sample, c_prompt, jnp.zeros((8 - batch, d), F32)], axis=0)
    prompt_mod_blk = nb // 8
    mod = ada_mod(c_all, w_ada, b_ada)

    alog_row = _head_row(a_log, 0)
    dtb_row = _head_row(dt_bias, 0)
    w_o_bf = w_o.astype(BF16)
    wr_pad = jnp.zeros((depth, d, LANES), F32).at[:, :, :N_EXPERTS].set(w_router)
    br_pad = jnp.zeros((depth, 1, LANES), F32).at[:, 0, :N_EXPERTS].set(b_router)
    sp_t = jnp.swapaxes(state_pool, 1, 2)
    ssc_t = jnp.swapaxes(state_shortconv, 1, 2)
    sq_t = jnp.swapaxes(state_qkv_conv, 1, 2)

    nb_max = (total * TOP_K + N_EXPERTS * (MOE_TM - 1)) // MOE_TM
    npad = nb_max * MOE_TM
    tn_gu, tn_dn = 1024, d
    w_in_bf = w_in.astype(BF16)

    pools_p, pools_s, scs_p, scs_s, qkvs_p, qkvs_s, deltas_p = ([] for _ in range(7))
    delta_s_all = None
    for l in range(depth):
        proj_p, ab_p = in_proj(x_all, mod, norm1, w_in_bf, l, row0=0, rows=n_prompt, tm=tm_in, tn=1024,
                               per_row=False, rows_per_batch=seq, mod_row_blk=prompt_mod_blk)
        proj_s, ab_s = in_proj(x_all, mod, norm1, w_in_bf, l, row0=n_prompt, rows=nb, tm=nb, tn=1024,
                               per_row=True, rows_per_batch=nb, mod_row_blk=0)
        ya_p = pool_prompt(proj_p, pool_w, pool_scale, l, batch=batch, seq=seq)
        yb_p, sc_new_p = sc_prompt(proj_p, sc_conv_w, l, batch=batch, seq=seq, width=sc_width, col0=col_sc)
        yc_p, s_new_p = gdn_prompt(proj_p, ab_p, qkv_conv_w, alog_row, dtb_row, onorm_w, l,
                                   batch=batch, seq=seq, col0=col_gdn)
        ya_s, yb_s, cx_s, qn, kn, vv, eg, beta = mix_sample(
            proj_s, ab_s, sp_t[l], ssc_t[l], sq_t[l], pool_w, pool_scale, sc_conv_w, qkv_conv_w,
            alog_row, dtb_row, l)
        yc_s, delta_s_all = delta_sample(qn, kn, vv, eg, beta, proj_s, onorm_w, state_delta, l,
                                         zcol_blk=(col_gdn + 3 * gdn_width) // gdn_width, prev=delta_s_all)

        p3 = proj_p.reshape(batch, seq, -1)
        pools_p.append(p3[:, seq - 15:, :pool_width])
        scs_p.append(sc_new_p)
        qkvs_p.append(p3[:, seq - 3:, col_gdn:col_gdn + 3 * gdn_width])
        deltas_p.append(s_new_p)
        pools_s.append(jnp.concatenate([state_pool[l][:, 1:], proj_s[:, None, :pool_width]], axis=1))
        scs_s.append(jnp.concatenate([state_shortconv[l][:, 1:], cx_s[:, None, :]], axis=1))
        qkvs_s.append(jnp.concatenate(
            [state_qkv_conv[l][:, 1:], proj_s[:, None, col_gdn:col_gdn + 3 * gdn_width]], axis=1))

        shared = o_proj(ya_p, yb_p, yc_p, x_all, mod, norm2, w_o_bf, wr_pad, br_pad, l,
                        row0=0, rows=n_prompt, tm=tm_o, per_row=False, rows_per_batch=seq,
                        mod_row_blk=prompt_mod_blk, total_rows=total)
        x1, xu, topi, topp = o_proj(ya_s, yb_s, yc_s, x_all, mod, norm2, w_o_bf, wr_pad, br_pad, l,
                                    row0=n_prompt, rows=nb, tm=nb, per_row=True, rows_per_batch=nb,
                                    mod_row_blk=0, total_rows=total, prev=shared)

        rank, cnt = route_rank(topi)
        counts = cnt[0, :N_EXPERTS]
        work_gu, row_off = make_work_list(counts, d // tn_gu, nb_max)
        work_dn, _ = make_work_list(counts, d // tn_dn, nb_max)
        ti4 = topi[:, :TOP_K]
        pos_flat = (_table_lookup(row_off, ti4) + rank[:, :TOP_K]).reshape(-1).astype(I32)
        xs = dispatch(pos_flat, xu, jnp.zeros((npad, d // 2), U32))
        act = moe_gate_up(work_gu, xs, w_gate_up, b_gate_up, l, tn=tn_gu)
        y_sorted = moe_down(work_dn, act, w_down, b_down, l, tn=tn_dn)
        x_all = combine(pos_flat, y_sorted, topp, x1, mod, l, n_prompt=n_prompt,
                        rows_per_batch=seq, prompt_mod_blk=prompt_mod_blk)

    y_p, y_s = final_norm_split(x_all, final_norm, n_prompt=n_prompt, tb=nb)
    return (y_p.reshape(batch, seq, d), y_s.reshape(nb, 1, d),
            jnp.stack(pools_p), jnp.stack(pools_s), jnp.stack(scs_p), jnp.stack(scs_s),
            jnp.stack(qkvs_p), jnp.stack(qkvs_s), jnp.stack(deltas_p), delta_s_all)
```

```python
import functools

import jax
import jax.numpy as jnp
from jax import lax
from jax.experimental import pallas as pl
from jax.experimental.pallas import tpu as pltpu

F32 = jnp.float32
BF16 = jnp.bfloat16
I32 = jnp.int32
U32 = jnp.uint32

EPS = 1e-6
POOL_WINDOWS = (2, 4, 8, 16)
POOL_GROUP = 128
GDN_HEADS = 8
GDN_DIM = 128
GDN_CHUNK = 64
N_EXPERTS = 32
TOP_K = 4
SWIGLU_LIMIT = 7.0
SWIGLU_ALPHA = 1.702
LANES = 128
MOE_BM = 512
MOE_HALF = MOE_BM // 2
VMEM_LIMIT = 56 * 1024 * 1024


def _cp(sem):
    return pltpu.CompilerParams(dimension_semantics=sem, vmem_limit_bytes=VMEM_LIMIT)


def _silu(x):
    return x * jax.nn.sigmoid(x)


def _softplus(x):
    return jnp.maximum(x, 0.0) + jnp.log1p(jnp.exp(-jnp.abs(x)))


def _bdot(a, b):
    return jnp.dot(a.astype(BF16), b.astype(BF16), preferred_element_type=F32)


def _shift_rows(x, s, row):
    return jnp.where(row >= s, pltpu.roll(x, s, 0), 0.0)


def _ada_kernel(c_ref, w_ref, b_ref, o_ref):
    o_ref[...] = _bdot(_silu(c_ref[...]), w_ref[...]) + b_ref[...]


def ada_mod(c_all, w_ada, b_ada, tn=1024):
    depth, d, n = w_ada.shape
    rows = c_all.shape[0]
    return pl.pallas_call(
        _ada_kernel,
        out_shape=jax.ShapeDtypeStruct((depth, rows, n), F32),
        grid=(depth, n // tn),
        in_specs=[
            pl.BlockSpec((rows, d), lambda l, j: (0, 0)),
            pl.BlockSpec((None, d, tn), lambda l, j: (l, 0, j)),
            pl.BlockSpec((None, 1, tn), lambda l, j: (l, 0, j)),
        ],
        out_specs=pl.BlockSpec((None, rows, tn), lambda l, j: (l, 0, j)),
        compiler_params=_cp(("arbitrary", "arbitrary")),
        name="ada_mod",
    )(c_all, w_ada, b_ada.reshape(depth, 1, n))


def _mod_rows(ref, per_row, brow):
    if per_row:
        return ref[...]
    return ref[pl.ds(brow, 1), :]


def _inproj_kernel(x_ref, sh_ref, sc_ref, n_ref, w_ref, wab_ref, o_ref, oab_ref, h_scr,
                   *, per_row, blocks_per_batch, n_ab):
    i = pl.program_id(0)
    j = pl.program_id(1)

    @pl.when(j == 0)
    def _():
        x = x_ref[...]
        y = x * lax.rsqrt(jnp.mean(x * x, -1, keepdims=True) + EPS) * n_ref[...]
        brow = i // blocks_per_batch
        h = y * (1.0 + _mod_rows(sc_ref, per_row, brow)) + _mod_rows(sh_ref, per_row, brow)
        hb = h.astype(BF16)
        h_scr[...] = hb
        col = lax.broadcasted_iota(I32, wab_ref.shape, 1)
        wab = jnp.where(col < n_ab, wab_ref[...], 0.0)
        oab_ref[...] = _bdot(hb, wab)

    o_ref[...] = jnp.dot(h_scr[...], w_ref[...], preferred_element_type=F32)


def in_proj(x_all, mod, norm_w, w_in, layer, *, row0, rows, tm, tn, per_row, rows_per_batch, mod_row_blk):
    depth, d, n_cols = w_in.shape
    n_main = (n_cols // LANES) * LANES
    n_ab = n_cols - n_main
    mrows = tm if per_row else 8
    mod_spec = lambda chunk: pl.BlockSpec((None, mrows, d), lambda i, j: (layer, mod_row_blk, chunk))
    kern = functools.partial(_inproj_kernel, per_row=per_row,
                             blocks_per_batch=max(rows_per_batch // tm, 1), n_ab=n_ab)
    return pl.pallas_call(
        kern,
        out_shape=(jax.ShapeDtypeStruct((rows, n_main), F32), jax.ShapeDtypeStruct((rows, LANES), F32)),
        grid=(rows // tm, n_main // tn),
        in_specs=[
            pl.BlockSpec((tm, d), lambda i, j: (row0 // tm + i, 0)),
            mod_spec(0), mod_spec(1),
            pl.BlockSpec((None, 1, d), lambda i, j: (layer, 0, 0)),
            pl.BlockSpec((None, d, tn), lambda i, j: (layer, 0, j)),
            pl.BlockSpec((None, d, LANES), lambda i, j: (layer, 0, n_main // LANES)),
        ],
        out_specs=(pl.BlockSpec((tm, tn), lambda i, j: (i, j)),
                   pl.BlockSpec((tm, LANES), lambda i, j: (i, 0))),
        scratch_shapes=[pltpu.VMEM((tm, d), BF16)],
        compiler_params=_cp(("arbitrary", "arbitrary")),
        name="in_proj",
    )(x_all, mod, mod, norm_w.reshape(depth, 1, d), w_in, w_in)


def _pool_p_kernel(u_ref, w_ref, s_ref, o_ref):
    g = pl.program_id(1)
    u = u_ref[...]
    row = lax.broadcasted_iota(I32, u.shape, 0)
    w2 = u + _shift_rows(u, 1, row)
    w4 = w2 + _shift_rows(w2, 2, row)
    w8 = w4 + _shift_rows(w4, 4, row)
    w16 = w8 + _shift_rows(w8, 8, row)
    wsum = jnp.where(g == 0, w2, jnp.where(g == 1, w4, jnp.where(g == 2, w8, w16)))
    win = jnp.left_shift(2, g)
    cnt = jnp.minimum(row + 1, win).astype(F32)
    d = wsum / cnt - u
    o_ref[...] = (_bdot(d, w_ref[...]) * s_ref[...]).astype(o_ref.dtype)


def pool_prompt(proj, pool_w, pool_scale, layer, *, batch, seq):
    depth = pool_w.shape[0]
    ng = len(POOL_WINDOWS)
    return pl.pallas_call(
        _pool_p_kernel,
        out_shape=jax.ShapeDtypeStruct((batch * seq, ng * POOL_GROUP), BF16),
        grid=(batch, ng),
        in_specs=[
            pl.BlockSpec((seq, POOL_GROUP), lambda b, g: (b, g)),
            pl.BlockSpec((None, None, POOL_GROUP, POOL_GROUP), lambda b, g: (layer, g, 0, 0)),
            pl.BlockSpec((None, 1, POOL_GROUP), lambda b, g: (layer, 0, g)),
        ],
        out_specs=pl.BlockSpec((seq, POOL_GROUP), lambda b, g: (b, g)),
        compiler_params=_cp(("arbitrary", "arbitrary")),
        name="pool_prompt",
    )(proj, pool_w, pool_scale.reshape(depth, 1, ng * POOL_GROUP))


def _sc_p_kernel(x_ref, b_ref, c_ref, w_ref, o_ref, ns_ref):
    cx = c_ref[...] * x_ref[...]
    row = lax.broadcasted_iota(I32, cx.shape, 0)
    w = w_ref[...]
    y = _shift_rows(cx, 2, row) * w[0:1] + _shift_rows(cx, 1, row) * w[1:2] + cx * w[2:3]
    o_ref[...] = (b_ref[...] * y).astype(o_ref.dtype)
    n = cx.shape[0]
    ns_ref[...] = cx[n - 2:n, :]


def sc_prompt(proj, sc_conv_w, layer, *, batch, seq, width, col0, tc=256):
    nct = width // tc
    cb = col0 // tc
    return pl.pallas_call(
        _sc_p_kernel,
        out_shape=(jax.ShapeDtypeStruct((batch * seq, width), BF16),
                   jax.ShapeDtypeStruct((batch, 2, width), F32)),
        grid=(batch, nct),
        in_specs=[
            pl.BlockSpec((seq, tc), lambda b, c: (b, cb + c)),
            pl.BlockSpec((seq, tc), lambda b, c: (b, cb + nct + c)),
            pl.BlockSpec((seq, tc), lambda b, c: (b, cb + 2 * nct + c)),
            pl.BlockSpec((None, 3, tc), lambda b, c: (layer, 0, c)),
        ],
        out_specs=(pl.BlockSpec((seq, tc), lambda b, c: (b, c)),
                   pl.BlockSpec((None, 2, tc), lambda b, c: (b, 0, c))),
        compiler_params=_cp(("arbitrary", "arbitrary")),
        name="sc_prompt",
    )(proj, proj, proj, sc_conv_w)


def _inv_unit_lower_minus_eye(lms):
    c = lms[0].shape[0]
    ps = [-lm for lm in lms]
    rs = list(ps)
    steps = max(c.bit_length() - 2, 0)
    for _ in range(steps):
        ps = [_bdot(p, p) for p in ps]
        rs = [r + p + _bdot(r, p) for r, p in zip(rs, ps)]
    return rs


def _gdn_p_kernel(qc_ref, kc_ref, vc_ref, qp_ref, kp_ref, vp_ref, z_ref, ab_ref,
                  cwq_ref, cwk_ref, cwv_ref, alog_ref, dtb_ref, on_ref,
                  o_ref, sout_ref, s_scr):
    n = pl.program_id(1)
    c = qc_ref.shape[0]
    hd = GDN_DIM

    @pl.when(n == 0)
    def _():
        s_scr[...] = jnp.zeros(s_scr.shape, F32)

    has_prev = n > 0

    def conv_silu(cur_ref, prev_ref, w_ref):
        prev = jnp.where(has_prev, prev_ref[...], 0.0)
        xx = jnp.concatenate([prev, cur_ref[...]], axis=0)
        w = w_ref[...]
        y = (xx[5:5 + c] * w[0:1] + xx[6:6 + c] * w[1:2]
             + xx[7:7 + c] * w[2:3] + xx[8:8 + c] * w[3:4])
        return _silu(y)

    q = conv_silu(qc_ref, qp_ref, cwq_ref)
    k = conv_silu(kc_ref, kp_ref, cwk_ref)
    v = conv_silu(vc_ref, vp_ref, cwv_ref)
    z = z_ref[...]

    ab = ab_ref[...]
    g_all = -jnp.exp(alog_ref[...]) * _softplus(ab + dtb_ref[...])
    beta_all = jax.nn.sigmoid(ab)
    row = lax.broadcasted_iota(I32, g_all.shape, 0)
    gc = g_all
    s = 1
    while s < c:
        gc = gc + _shift_rows(gc, s, row)
        s *= 2
    gct = gc.T
    eg_all = jnp.exp(gc)

    ri = lax.broadcasted_iota(I32, (c, c), 0)
    ci = lax.broadcasted_iota(I32, (c, c), 1)
    incl = ri >= ci
    strict = ri > ci
    on = on_ref[...]

    heads = range(GDN_HEADS)
    nt = (((1,), (1,)), ((), ()))
    tn_dims = (((0,), (0,)), ((), ()))
    qns, kns, rhss, lms, aintras = [], [], [], [], []
    for h in heads:
        sl = slice(h * hd, (h + 1) * hd)
        qh, kh, vh = q[:, sl], k[:, sl], v[:, sl]
        qn = qh * lax.rsqrt(jnp.sum(qh * qh, -1, keepdims=True) + EPS) * (hd ** -0.5)
        kn = kh * lax.rsqrt(jnp.sum(kh * kh, -1, keepdims=True) + EPS)
        gcol = gc[:, h:h + 1]
        grow = gct[h:h + 1, :]
        bcol = beta_all[:, GDN_HEADS + h:GDN_HEADS + h + 1]
        decay = jnp.where(incl, jnp.exp(jnp.minimum(gcol - grow, 0.0)), 0.0)
        kb = kn * bcol
        knb = kn.astype(BF16)
        kk = lax.dot_general(kb.astype(BF16), knb, nt, preferred_element_type=F32)
        qk = lax.dot_general(qn.astype(BF16), knb, nt, preferred_element_type=F32)
        lms.append(jnp.where(strict, kk * decay, 0.0))
        aintras.append(jnp.where(incl, qk * decay, 0.0))
        rhss.append(jnp.concatenate([vh * bcol, kb * eg_all[:, h:h + 1]], axis=-1))
        qns.append(qn)
        kns.append(kn)
    rs = _inv_unit_lower_minus_eye(lms)
    sols = [rhs + _bdot(r, rhs) for r, rhs in zip(rs, rhss)]
    for h in heads:
        sl = slice(h * hd, (h + 1) * hd)
        gcol = gc[:, h:h + 1]
        u_val, k_cum = sols[h][:, :hd], sols[h][:, hd:]
        s_old = s_scr[h]
        v_new = u_val - _bdot(k_cum, s_old)
        o = _bdot(qns[h] * eg_all[:, h:h + 1], s_old) + _bdot(aintras[h], v_new)
        glast = gc[c - 1:c, h:h + 1]
        kd = kns[h] * jnp.exp(glast - gcol)
        s_scr[h] = s_old * jnp.exp(glast) + lax.dot_general(
            kd.astype(BF16), v_new.astype(BF16), tn_dims, preferred_element_type=F32)
        o = o * lax.rsqrt(jnp.mean(o * o, -1, keepdims=True) + EPS) * on
        o_ref[:, sl] = (o * _silu(z[:, sl])).astype(o_ref.dtype)

    @pl.when(n == pl.num_programs(1) - 1)
    def _():
        sout_ref[...] = s_scr[...]


def _head_row(v, offset):
    depth, h = v.shape
    return jnp.zeros((depth, 1, LANES), F32).at[:, 0, offset:offset + h].set(v.astype(F32))


def gdn_prompt(proj, ab, qkv_conv_w, alog_row, dtb_row, onorm_w, layer, *, batch, seq, col0):
    depth = qkv_conv_w.shape[0]
    wdt = GDN_HEADS * GDN_DIM
    c = GDN_CHUNK
    nchunk = seq // c
    cb = col0 // wdt
    cur = lambda t: pl.BlockSpec((c, wdt), lambda b, n: (b * nchunk + n, cb + t))
    prev = lambda t: pl.BlockSpec(
        (8, wdt), lambda b, n: (jnp.maximum((b * nchunk + n) * (c // 8) - 1, 0), cb + t))
    cw = lambda t: pl.BlockSpec((None, 4, wdt), lambda b, n: (layer, 0, t))
    row = pl.BlockSpec((None, 1, LANES), lambda b, n: (layer, 0, 0))
    return pl.pallas_call(
        _gdn_p_kernel,
        out_shape=(jax.ShapeDtypeStruct((batch * seq, wdt), BF16),
                   jax.ShapeDtypeStruct((batch, GDN_HEADS, GDN_DIM, GDN_DIM), F32)),
        grid=(batch, nchunk),
        in_specs=[cur(0), cur(1), cur(2), prev(0), prev(1), prev(2), cur(3),
                  pl.BlockSpec((c, LANES), lambda b, n: (b * nchunk + n, 0)),
                  cw(0), cw(1), cw(2), row, row, row],
        out_specs=(pl.BlockSpec((c, wdt), lambda b, n: (b * nchunk + n, 0)),
                   pl.BlockSpec((None, GDN_HEADS, GDN_DIM, GDN_DIM), lambda b, n: (b, 0, 0, 0))),
        scratch_shapes=[pltpu.VMEM((GDN_HEADS, GDN_DIM, GDN_DIM), F32)],
        compiler_params=_cp(("arbitrary", "arbitrary")),
        name="gdn_prompt",
    )(proj, proj, proj, proj, proj, proj, proj, ab,
      qkv_conv_w, qkv_conv_w, qkv_conv_w, alog_row, dtb_row,
      onorm_w.reshape(depth, 1, GDN_DIM))


def _mix_s_kernel(proj_ref, ab_ref, sp_ref, ssc_ref, sq_ref, pw_ref, ps_ref, scw_ref, cw_ref,
                  alog_ref, dtb_ref,
                  ya_ref, yb_ref, cx_ref, q_ref, k_ref, v_ref, eg_ref, beta_ref,
                  *, pool_w, sc_w, gdn_w):
    for g, win in enumerate(POOL_WINDOWS):
        sl = slice(g * POOL_GROUP, (g + 1) * POOL_GROUP)
        u = proj_ref[:, sl]
        acc = u
        nbuf = sp_ref.shape[0]
        for r in range(nbuf - (win - 1), nbuf):
            acc = acc + sp_ref[r, :, sl]
        d = acc / float(win) - u
        ya_ref[:, sl] = (_bdot(d, pw_ref[g]) * ps_ref[:, sl]).astype(ya_ref.dtype)
    c0 = pool_w
    xs = proj_ref[:, c0:c0 + sc_w]
    bg = proj_ref[:, c0 + sc_w:c0 + 2 * sc_w]
    cg = proj_ref[:, c0 + 2 * sc_w:c0 + 3 * sc_w]
    cx = cg * xs
    w = scw_ref[...]
    y = ssc_ref[0] * w[0:1] + ssc_ref[1] * w[1:2] + cx * w[2:3]
    yb_ref[...] = (bg * y).astype(yb_ref.dtype)
    cx_ref[...] = cx
    c1 = c0 + 3 * sc_w
    hd = GDN_DIM
    outs = (q_ref, k_ref, v_ref)
    for t in range(3):
        new = proj_ref[:, c1 + t * gdn_w:c1 + (t + 1) * gdn_w]
        ws = slice(t * gdn_w, (t + 1) * gdn_w)
        y = (sq_ref[0, :, ws] * cw_ref[0:1, ws] + sq_ref[1, :, ws] * cw_ref[1:2, ws]
             + sq_ref[2, :, ws] * cw_ref[2:3, ws] + new * cw_ref[3:4, ws])
        y = _silu(y)
        for h in range(GDN_HEADS):
            sl = slice(h * hd, (h + 1) * hd)
            yh = y[:, sl]
            if t == 0:
                yh = yh * lax.rsqrt(jnp.sum(yh * yh, -1, keepdims=True) + EPS) * (hd ** -0.5)
            elif t == 1:
                yh = yh * lax.rsqrt(jnp.sum(yh * yh, -1, keepdims=True) + EPS)
            outs[t][:, sl] = yh
    ab = ab_ref[...]
    eg_ref[...] = jnp.exp(-jnp.exp(alog_ref[...]) * _softplus(ab + dtb_ref[...]))
    beta_ref[...] = jax.nn.sigmoid(ab)


def mix_sample(proj, ab, sp_t, ssc_t, sq_t, pool_w, pool_scale, sc_conv_w, qkv_conv_w,
               alog_row, dtb_row, layer):
    nb = proj.shape[0]
    depth = pool_w.shape[0]
    pw = pool_scale.shape[1]
    scw = sc_conv_w.shape[2]
    gw = qkv_conv_w.shape[2] // 3
    full = lambda a: pl.BlockSpec(a.shape, lambda i: (0,) * a.ndim)
    lay = lambda a: pl.BlockSpec((None,) + a.shape[1:], lambda i: (layer,) + (0,) * (a.ndim - 1))
    ps3 = pool_scale.reshape(depth, 1, pw)
    kern = functools.partial(_mix_s_kernel, pool_w=pw, sc_w=scw, gdn_w=gw)
    shp = lambda w, dt: jax.ShapeDtypeStruct((nb, w), dt)
    outs = (shp(pw, BF16), shp(scw, BF16), shp(scw, F32), shp(gw, F32), shp(gw, F32), shp(gw, F32),
            shp(LANES, F32), shp(LANES, F32))
    return pl.pallas_call(
        kern,
        out_shape=outs,
        grid=(1,),
        in_specs=[full(proj), full(ab), full(sp_t), full(ssc_t), full(sq_t),
                  lay(pool_w), lay(ps3), lay(sc_conv_w), lay(qkv_conv_w), lay(alog_row), lay(dtb_row)],
        out_specs=tuple(pl.BlockSpec(o.shape, lambda i: (0, 0)) for o in outs),
        compiler_params=_cp(("arbitrary",)),
        name="mix_sample",
    )(proj, ab, sp_t, ssc_t, sq_t, pool_w, ps3, sc_conv_w, qkv_conv_w, alog_row, dtb_row)


def _delta_s_kernel(q_ref, k_ref, v_ref, eg_ref, beta_ref, z_ref, on_ref, s_ref, *rest, bb):
    o_ref, so_ref = rest[-2:]
    i = pl.program_id(0)
    hd = GDN_DIM
    on = on_ref[...]
    tn_dims = (((0,), (0,)), ((), ()))
    heads = range(GDN_HEADS)

    def body(r, carry):
        t = i * bb + r
        qrow = q_ref[pl.ds(t, 1), :]
        krow = k_ref[pl.ds(t, 1), :]
        vrow = v_ref[pl.ds(t, 1), :]
        egrow = eg_ref[pl.ds(t, 1), :]
        brow = beta_ref[pl.ds(t, 1), :]
        zrow = z_ref[pl.ds(t, 1), :]
        ress = []
        for h in heads:
            sl = slice(h * hd, (h + 1) * hd)
            kq = jnp.concatenate([krow[:, sl], qrow[:, sl], jnp.zeros((6, hd), F32)], axis=0)
            ress.append(_bdot(kq, s_ref[r, h]))
        upds = []
        for h in heads:
            sl = slice(h * hd, (h + 1) * hd)
            qh, kh, vh = qrow[:, sl], krow[:, sl], vrow[:, sl]
            eg = egrow[:, h:h + 1]
            bt = brow[:, GDN_HEADS + h:GDN_HEADS + h + 1]
            delta = (vh - eg * ress[h][0:1]) * bt
            qk = jnp.sum(qh * kh, -1, keepdims=True)
            o = eg * ress[h][1:2] + qk * delta
            o = o * lax.rsqrt(jnp.mean(o * o, -1, keepdims=True) + EPS) * on
            o_ref[r, :, sl] = o * _silu(zrow[:, sl])
            khi = kh.astype(BF16).astype(F32)
            dhi = delta.astype(BF16).astype(F32)
            zpad = jnp.zeros((13, hd), F32)
            lhs = jnp.concatenate([khi, kh - khi, khi, zpad], axis=0).astype(BF16)
            rhs = jnp.concatenate([dhi, dhi, delta - dhi, zpad], axis=0).astype(BF16)
            upds.append(lax.dot_general(lhs, rhs, tn_dims, preferred_element_type=F32))
        for h in heads:
            so_ref[r, h] = s_ref[r, h] * egrow[:, h:h + 1] + upds[h]
        return carry

    lax.fori_loop(0, bb, body, 0)


def delta_sample(qn, kn, vv, eg, beta, proj, onorm_w, state_delta, layer, *, zcol_blk, prev=None, bb=8):
    nb, gw = qn.shape
    depth = onorm_w.shape[0]
    full = lambda a: pl.BlockSpec(a.shape, lambda i: (0, 0))
    sblk = (None, bb, GDN_HEADS, GDN_DIM, GDN_DIM)
    in_specs = [full(qn), full(kn), full(vv), full(eg), full(beta),
                pl.BlockSpec((nb, gw), lambda i: (0, zcol_blk)),
                pl.BlockSpec((None, 1, GDN_DIM), lambda i: (layer, 0, 0)),
                pl.BlockSpec(sblk, lambda i: (layer, i, 0, 0, 0))]
    args = [qn, kn, vv, eg, beta, proj, onorm_w.reshape(depth, 1, GDN_DIM), state_delta]
    aliases = {}
    if prev is not None:
        in_specs.append(pl.BlockSpec(memory_space=pl.ANY))
        args.append(prev)
        aliases = {len(args) - 1: 1}
    o, s_new = pl.pallas_call(
        functools.partial(_delta_s_kernel, bb=bb),
        out_shape=(jax.ShapeDtypeStruct((nb, 1, gw), F32),
                   jax.ShapeDtypeStruct(state_delta.shape, F32)),
        grid=(nb // bb,),
        in_specs=in_specs,
        out_specs=(pl.BlockSpec((bb, 1, gw), lambda i: (i, 0, 0)),
                   pl.BlockSpec(sblk, lambda i: (layer, i, 0, 0, 0))),
        input_output_aliases=aliases,
        compiler_params=_cp(("arbitrary",)),
        name="delta_sample",
    )(*args)
    return o.reshape(nb, gw), s_new


def _pack_halves(h):
    half = h.shape[1] // 2
    lo = pltpu.bitcast(h[:, :half].astype(BF16).astype(F32), U32)
    hi = pltpu.bitcast(h[:, half:].astype(BF16).astype(F32), U32)
    return (lo >> 16) | (hi & jnp.uint32(0xFFFF0000))


def _unpack_halves(xu):
    lo = pltpu.bitcast(xu << 16, F32).astype(BF16)
    hi = pltpu.bitcast(xu & jnp.uint32(0xFFFF0000), F32).astype(BF16)
    return lo, hi


def _oproj_kernel(*refs, per_row, blocks_per_batch, n_alias):
    (ya_ref, yb_ref, yc_ref, x_ref, g1_ref, sh_ref, sc_ref, n_ref,
     woa_ref, wob_ref, woc_ref, wr_ref, br_ref) = refs[:13]
    x1_ref, xu_ref, ti_ref, tp_ref = refs[13 + n_alias:13 + n_alias + 4]
    i = pl.program_id(0)
    brow = i // blocks_per_batch
    acc = (jnp.dot(ya_ref[...].astype(BF16), woa_ref[...], preferred_element_type=F32)
           + jnp.dot(yb_ref[...].astype(BF16), wob_ref[...], preferred_element_type=F32)
           + jnp.dot(yc_ref[...].astype(BF16), woc_ref[...], preferred_element_type=F32))
    x1 = x_ref[...] + _mod_rows(g1_ref, per_row, brow) * acc
    x1_ref[...] = x1
    y = x1 * lax.rsqrt(jnp.mean(x1 * x1, -1, keepdims=True) + EPS) * n_ref[...]
    h = y * (1.0 + _mod_rows(sc_ref, per_row, brow)) + _mod_rows(sh_ref, per_row, brow)
    xu_ref[...] = _pack_halves(h)
    logits = _bdot(h, wr_ref[...]) + br_ref[...]
    lane = lax.broadcasted_iota(I32, logits.shape, 1)
    neg = jnp.float32(-jnp.inf)
    cur = jnp.where(lane < N_EXPERTS, logits, neg)
    ti = jnp.zeros(logits.shape, I32)
    tv = jnp.full(logits.shape, neg, F32)
    for kk in range(TOP_K):
        m = jnp.max(cur, -1, keepdims=True)
        idx = jnp.min(jnp.where(cur == m, lane, LANES), -1, keepdims=True)
        ti = jnp.where(lane == kk, idx, ti)
        tv = jnp.where(lane == kk, m, tv)
        cur = jnp.where(lane == idx, neg, cur)
    e = jnp.exp(tv - jnp.max(tv, -1, keepdims=True))
    ti_ref[...] = ti
    tp_ref[...] = e / jnp.sum(e, -1, keepdims=True)


def o_proj(ya, yb, yc, x_all, mod, norm_w, w_o_bf, w_router_pad, b_router_pad, layer, *,
           row0, rows, tm, per_row, rows_per_batch, mod_row_blk, total_rows, prev=None):
    depth, d, _ = w_o_bf.shape
    wa, wb, wc = ya.shape[1], yb.shape[1], yc.shape[1]
    mrows = tm if per_row else 8
    rb = row0 // tm
    mod_spec = lambda chunk: pl.BlockSpec((None, mrows, d), lambda i: (layer, mod_row_blk, chunk))
    n_alias = 0 if prev is None else 4
    kern = functools.partial(_oproj_kernel, per_row=per_row,
                             blocks_per_batch=max(rows_per_batch // tm, 1), n_alias=n_alias)
    in_specs = [
        pl.BlockSpec((tm, wa), lambda i: (i, 0)),
        pl.BlockSpec((tm, wb), lambda i: (i, 0)),
        pl.BlockSpec((tm, wc), lambda i: (i, 0)),
        pl.BlockSpec((tm, d), lambda i: (rb + i, 0)),
        mod_spec(2), mod_spec(3), mod_spec(4),
        pl.BlockSpec((None, 1, d), lambda i: (layer, 0, 0)),
        pl.BlockSpec((None, wa, d), lambda i: (layer, 0, 0)),
        pl.BlockSpec((None, wb, d), lambda i: (layer, wa // wb, 0)),
        pl.BlockSpec((None, wc, d), lambda i: (layer, (wa + wb) // wc, 0)),
        pl.BlockSpec((None, d, LANES), lambda i: (layer, 0, 0)),
        pl.BlockSpec((None, 1, LANES), lambda i: (layer, 0, 0)),
    ]
    args = [ya, yb, yc, x_all, mod, mod, mod, norm_w.reshape(depth, 1, d),
            w_o_bf, w_o_bf, w_o_bf, w_router_pad, b_router_pad]
    aliases = {}
    if prev is not None:
        in_specs += [pl.BlockSpec(memory_space=pl.ANY)] * 4
        args += list(prev)
        aliases = {13 + t: t for t in range(4)}
    out_shape = (jax.ShapeDtypeStruct((total_rows, d), F32),
                 jax.ShapeDtypeStruct((total_rows, d // 2), U32),
                 jax.ShapeDtypeStruct((total_rows, LANES), I32),
                 jax.ShapeDtypeStruct((total_rows, LANES), F32))
    out_specs = (pl.BlockSpec((tm, d), lambda i: (rb + i, 0)),
                 pl.BlockSpec((tm, d // 2), lambda i: (rb + i, 0)),
                 pl.BlockSpec((tm, LANES), lambda i: (rb + i, 0)),
                 pl.BlockSpec((tm, LANES), lambda i: (rb + i, 0)))
    return pl.pallas_call(
        kern, out_shape=out_shape, grid=(rows // tm,),
        in_specs=in_specs, out_specs=out_specs,
        input_output_aliases=aliases,
        compiler_params=_cp(("arbitrary",)),
        name="o_proj",
    )(*args)


def _rank_kernel(ti_ref, rank_ref, cnt_ref, run_scr):
    i = pl.program_id(0)

    @pl.when(i == 0)
    def _():
        run_scr[...] = jnp.zeros(run_scr.shape, F32)

    ti = ti_ref[...]
    tb = ti.shape[0]
    lane = lax.broadcasted_iota(I32, ti.shape, 1)
    sel = jnp.zeros(ti.shape, F32)
    for kk in range(TOP_K):
        sel = sel + jnp.where(lane == ti[:, kk:kk + 1], 1.0, 0.0)
    r = lax.broadcasted_iota(I32, (tb, tb), 0)
    q = lax.broadcasted_iota(I32, (tb, tb), 1)
    tri = jnp.where(r > q, 1.0, 0.0)
    before = _bdot(tri, sel) + run_scr[0:1, :]
    rank = jnp.zeros(ti.shape, F32)
    for kk in range(TOP_K):
        rk = jnp.sum(jnp.where(lane == ti[:, kk:kk + 1], before, 0.0), -1, keepdims=True)
        rank = jnp.where(lane == kk, rk, rank)
    rank_ref[...] = rank.astype(I32)
    total = run_scr[0:1, :] + jnp.sum(sel, 0, keepdims=True)
    run_scr[...] = jnp.broadcast_to(total, run_scr.shape)
    cnt_ref[...] = jnp.broadcast_to(total, cnt_ref.shape).astype(I32)


def route_rank(topi, tb=128):
    t = topi.shape[0]
    return pl.pallas_call(
        _rank_kernel,
        out_shape=(jax.ShapeDtypeStruct((t, LANES), I32), jax.ShapeDtypeStruct((8, LANES), I32)),
        grid=(t // tb,),
        in_specs=[pl.BlockSpec((tb, LANES), lambda i: (i, 0))],
        out_specs=(pl.BlockSpec((tb, LANES), lambda i: (i, 0)),
                   pl.BlockSpec((8, LANES), lambda i: (0, 0))),
        scratch_shapes=[pltpu.VMEM((8, LANES), F32)],
        compiler_params=_cp(("arbitrary",)),
        name="route_rank",
    )(topi)


def _dispatch_kernel(pos_ref, xu_ref, xs_in_ref, xs_ref, sem, *, tb):
    del xs_in_ref
    i = pl.program_id(0)
    base = i * tb * TOP_K

    def row_copy(r, kk):
        p = pos_ref[base + r * TOP_K + kk]
        return pltpu.make_async_copy(xu_ref.at[pl.ds(r, 1)], xs_ref.at[pl.ds(p, 1)], sem)

    def start(r, c):
        for kk in range(TOP_K):
            row_copy(r, kk).start()
        return c

    def wait(r, c):
        for kk in range(TOP_K):
            row_copy(r, kk).wait()
        return c

    lax.fori_loop(0, tb, start, 0, unroll=4)
    lax.fori_loop(0, tb, wait, 0, unroll=4)


def dispatch(pos_flat, xu, xs_zero, tb=128):
    t, w = xu.shape
    return pl.pallas_call(
        functools.partial(_dispatch_kernel, tb=tb),
        out_shape=jax.ShapeDtypeStruct(xs_zero.shape, xs_zero.dtype),
        grid_spec=pltpu.PrefetchScalarGridSpec(
            num_scalar_prefetch=1,
            grid=(t // tb,),
            in_specs=[pl.BlockSpec((tb, w), lambda i, pos: (i, 0)),
                      pl.BlockSpec(memory_space=pl.ANY)],
            out_specs=pl.BlockSpec(memory_space=pl.ANY),
            scratch_shapes=[pltpu.SemaphoreType.DMA(())],
        ),
        input_output_aliases={2: 0},
        compiler_params=_cp(("arbitrary",)),
        name="moe_dispatch",
    )(pos_flat, xu, xs_zero)


FLAG_VALID, FLAG_FIRST, FLAG_HALF, FLAG_SLOT, FLAG_NEXT = 1, 2, 4, 8, 16


def _expert_mm_kernel(we_ref, wj_ref, wi_ref, wfl_ref, wne_ref, wnj_ref, *refs, layer, tn, up_col0):
    gate_up = up_col0 is not None
    if gate_up:
        x_ref, w_hbm, bg_ref, bu_ref, o_ref, wf32, wbf, sem = refs
    else:
        x_ref, w_hbm, bd_ref, o_ref, wf32, wbf, sem = refs
    ntile = 2 if gate_up else 1
    w = pl.program_id(0)
    fl = wfl_ref[w]
    has = lambda bit: (fl & bit) != 0
    slot = jnp.where(has(FLAG_SLOT), 1, 0)

    def wcopy(e, j, sl, t):
        col = pl.multiple_of(j * tn + (up_col0 if t else 0), tn)
        return pltpu.make_async_copy(w_hbm.at[layer, e, :, pl.ds(col, tn)], wf32.at[sl, t], sem.at[sl, t])

    @pl.when(w == 0)
    def _():
        for t in range(ntile):
            wcopy(we_ref[0], wj_ref[0], 0, t).start()

    @pl.when(has(FLAG_FIRST))
    def _():
        for t in range(ntile):
            wcopy(we_ref[w], wj_ref[w], slot, t).wait()

        @pl.when(has(FLAG_NEXT))
        def _():
            for t in range(ntile):
                wcopy(wne_ref[w], wnj_ref[w], 1 - slot, t).start()

        for t in range(ntile):
            wbf[t] = wf32[slot, t].astype(BF16)

    def compute(m):
        if gate_up:
            lo, hi = _unpack_halves(x_ref[0:m, :])
            half = lo.shape[1]
            gate = (jnp.dot(lo, wbf[0, :half], preferred_element_type=F32)
                    + jnp.dot(hi, wbf[0, half:], preferred_element_type=F32) + bg_ref[...])
            up = (jnp.dot(lo, wbf[1, :half], preferred_element_type=F32)
                  + jnp.dot(hi, wbf[1, half:], preferred_element_type=F32) + bu_ref[...])
            gate = jnp.minimum(gate, SWIGLU_LIMIT)
            up = jnp.clip(up, -SWIGLU_LIMIT, SWIGLU_LIMIT)
            o_ref[0:m, :] = ((up + 1.0) * gate * jax.nn.sigmoid(SWIGLU_ALPHA * gate)).astype(o_ref.dtype)
        else:
            o_ref[0:m, :] = jnp.dot(x_ref[0:m, :], wbf[0], preferred_element_type=F32) + bd_ref[...]

    @pl.when(has(FLAG_VALID) & jnp.logical_not(has(FLAG_HALF)))
    def _():
        compute(MOE_BM)

    @pl.when(has(FLAG_VALID) & has(FLAG_HALF))
    def _():
        compute(MOE_HALF)


def _expert_mm(work, x, w, bias, layer, *, tn, gate_up, out_dtype):
    depth, ne, k, n_all = w.shape
    n_out = n_all // 2 if gate_up else n_all
    nj = n_out // tn
    npad = x.shape[0]
    nwork = work[0].shape[0]
    b4 = bias.reshape(depth, ne, 1, n_all)
    ntile = 2 if gate_up else 1
    x_spec = pl.BlockSpec((MOE_BM, x.shape[1]), lambda w_, we, wj, wi, *_: (wi[w_], 0))
    b_spec = lambda off: pl.BlockSpec((None, None, 1, tn), lambda w_, we, wj, *_: (layer, we[w_], 0, off + wj[w_]))
    in_specs = [x_spec, pl.BlockSpec(memory_space=pl.ANY), b_spec(0)] + ([b_spec(nj)] if gate_up else [])
    args = [x, w, b4] + ([b4] if gate_up else [])
    kern = functools.partial(_expert_mm_kernel, layer=layer, tn=tn, up_col0=n_out if gate_up else None)
    return pl.pallas_call(
        kern,
        out_shape=jax.ShapeDtypeStruct((npad, n_out), out_dtype),
        grid_spec=pltpu.PrefetchScalarGridSpec(
            num_scalar_prefetch=6,
            grid=(nwork,),
            in_specs=in_specs,
            out_specs=pl.BlockSpec((MOE_BM, tn), lambda w_, we, wj, wi, *_: (wi[w_], wj[w_])),
            scratch_shapes=[pltpu.VMEM((2, ntile, k, tn), F32), pltpu.VMEM((ntile, k, tn), BF16),
                            pltpu.SemaphoreType.DMA((2, ntile))],
        ),
        compiler_params=_cp(("arbitrary",)),
        name="moe_gate_up" if gate_up else "moe_down",
    )(*work, *args)


def moe_gate_up(work, xs, w_gate_up, b_gate_up, layer, *, tn):
    return _expert_mm(work, xs, w_gate_up, b_gate_up, layer, tn=tn, gate_up=True, out_dtype=BF16)


def moe_down(work, act, w_down, b_down, layer, *, tn):
    return _expert_mm(work, act, w_down, b_down, layer, tn=tn, gate_up=False, out_dtype=F32)


def make_work_lists(counts, njs, nb_max):
    n_half = (counts + MOE_HALF - 1) // MOE_HALF
    nb = (n_half + 1) // 2
    odd = n_half % 2
    blk_off = jnp.cumsum(nb) - nb
    total = jnp.sum(nb)
    nonempty = (nb > 0).astype(I32)
    groups_before = jnp.cumsum(nonempty) - nonempty
    ar = jnp.arange(N_EXPERTS, dtype=I32)
    later = (ar[None, :] > ar[:, None]) & (nb[None, :] > 0)
    next_e = jnp.min(jnp.where(later, ar[None, :], N_EXPERTS), axis=1)
    lists = []
    for nj in njs:
        start = nj * blk_off
        nvalid = nj * total
        w = jnp.arange(nj * nb_max, dtype=I32)
        wc = jnp.minimum(w, nvalid - 1)
        e = jnp.clip(jnp.sum((start[None, :] <= wc[:, None]).astype(I32), axis=1) - 1, 0, N_EXPERTS - 1)
        local = wc - _table_lookup(start, e)
        nbe = jnp.maximum(_table_lookup(nb, e), 1)
        j = local // nbe
        il = local % nbe
        valid = w < nvalid
        first = (il == 0) & valid
        half = (il == nbe - 1) & (_table_lookup(odd, e) == 1)
        slot = (_table_lookup(groups_before, e) * nj + j) % 2
        same_e = j + 1 < nj
        e_next = jnp.where(same_e, e, _table_lookup(next_e, e))
        j_next = jnp.where(same_e, j + 1, 0)
        has_next = e_next < N_EXPERTS
        flags = (FLAG_VALID * valid + FLAG_FIRST * first + FLAG_HALF * half + FLAG_SLOT * slot
                 + FLAG_NEXT * has_next).astype(I32)
        lists.append((e, j.astype(I32), (_table_lookup(blk_off, e) + il).astype(I32), flags,
                      jnp.minimum(e_next, N_EXPERTS - 1).astype(I32), j_next.astype(I32)))
    return lists, blk_off * MOE_BM


def _table_lookup(table, idx):
    onehot = idx[..., None] == jnp.arange(table.shape[0], dtype=I32)
    return jnp.sum(jnp.where(onehot, table, 0), axis=-1).astype(I32)


def _combine_kernel(pos_ref, y_ref, tp_ref, x1_ref, g2p_ref, g2s_ref, o_ref, buf, sem,
                    *, tb, n_prompt_blocks, blocks_per_batch):
    i = pl.program_id(0)
    slot = i % 2

    def row_copy(blk, sl, r, kk):
        p = pos_ref[(blk * tb + r) * TOP_K + kk]
        return pltpu.make_async_copy(y_ref.at[pl.ds(p, 1)], buf.at[sl, kk, pl.ds(r, 1)], sem.at[sl])

    def start_block(blk, sl):
        def body(r, c):
            for kk in range(TOP_K):
                row_copy(blk, sl, r, kk).start()
            return c
        lax.fori_loop(0, tb, body, 0, unroll=4)

    def wait_block(blk, sl):
        def body(r, c):
            for kk in range(TOP_K):
                row_copy(blk, sl, r, kk).wait()
            return c
        lax.fori_loop(0, tb, body, 0, unroll=4)

    @pl.when(i == 0)
    def _():
        start_block(0, 0)

    @pl.when(i + 1 < pl.num_programs(0))
    def _():
        start_block(i + 1, 1 - slot)

    wait_block(i, slot)
    tp = tp_ref[...]
    moe = buf[slot, 0] * tp[:, 0:1]
    for kk in range(1, TOP_K):
        moe = moe + buf[slot, kk] * tp[:, kk:kk + 1]
    brow = jnp.minimum(i // blocks_per_batch, 7)
    g2 = jnp.where(i >= n_prompt_blocks, g2s_ref[...], g2p_ref[pl.ds(brow, 1), :])
    o_ref[...] = x1_ref[...] + g2 * moe


def combine(pos_flat, y_sorted, topp, x1, mod, layer, *, n_prompt, rows_per_batch, prompt_mod_blk, tb=64):
    t, d = x1.shape
    npb = n_prompt // tb
    kern = functools.partial(_combine_kernel, tb=tb, n_prompt_blocks=npb,
                             blocks_per_batch=rows_per_batch // tb)
    return pl.pallas_call(
        kern,
        out_shape=jax.ShapeDtypeStruct((t, d), F32),
        grid_spec=pltpu.PrefetchScalarGridSpec(
            num_scalar_prefetch=1,
            grid=(t // tb,),
            in_specs=[
                pl.BlockSpec(memory_space=pl.ANY),
                pl.BlockSpec((tb, LANES), lambda i, pos: (i, 0)),
                pl.BlockSpec((tb, d), lambda i, pos: (i, 0)),
                pl.BlockSpec((None, 8, d), lambda i, pos: (layer, prompt_mod_blk, 5)),
                pl.BlockSpec((None, tb, d), lambda i, pos: (layer, jnp.maximum(i - npb, 0), 5)),
            ],
            out_specs=pl.BlockSpec((tb, d), lambda i, pos: (i, 0)),
            scratch_shapes=[pltpu.VMEM((2, TOP_K, tb, d), F32), pltpu.SemaphoreType.DMA((2,))],
        ),
        compiler_params=_cp(("arbitrary",)),
        name="moe_combine",
    )(pos_flat, y_sorted, topp, x1, mod, mod)


def _final_kernel(x_ref, w_ref, yp_ref, ys_ref, *, n_prompt_blocks):
    i = pl.program_id(0)
    x = x_ref[...]
    y = x * lax.rsqrt(jnp.mean(x * x, -1, keepdims=True) + EPS) * w_ref[...]

    @pl.when(i < n_prompt_blocks)
    def _():
        yp_ref[...] = y

    @pl.when(i >= n_prompt_blocks)
    def _():
        ys_ref[...] = y


def final_norm_split(x_all, w, *, n_prompt, tb):
    t, d = x_all.shape
    npb = n_prompt // tb
    return pl.pallas_call(
        functools.partial(_final_kernel, n_prompt_blocks=npb),
        out_shape=(jax.ShapeDtypeStruct((n_prompt, d), F32), jax.ShapeDtypeStruct((t - n_prompt, d), F32)),
        grid=(t // tb,),
        in_specs=[pl.BlockSpec((tb, d), lambda i: (i, 0)),
                  pl.BlockSpec((1, d), lambda i: (0, 0))],
        out_specs=(pl.BlockSpec((tb, d), lambda i: (jnp.minimum(i, npb - 1), 0)),
                   pl.BlockSpec((tb, d), lambda i: (jnp.maximum(i - npb, 0), 0))),
        compiler_params=_cp(("arbitrary",)),
        name="final_norm",
    )(x_all, w.reshape(1, d))


def kernel(x_prompt, x_sample, state_pool, state_shortconv, state_qkv_conv, state_delta,
           c_prompt, c_sample, norm1, norm2, w_ada, b_ada, w_in, pool_w, pool_scale,
           sc_conv_w, qkv_conv_w, a_log, dt_bias, onorm_w, w_o, w_router, b_router,
           w_gate_up, b_gate_up, w_down, b_down, final_norm):
    batch, seq, d = x_prompt.shape
    nb = x_sample.shape[0]
    depth = w_in.shape[0]
    n_prompt = batch * seq
    total = n_prompt + nb
    pool_width = pool_scale.shape[1]
    sc_width = sc_conv_w.shape[2]
    gdn_width = qkv_conv_w.shape[2] // 3
    col_sc = pool_width
    col_gdn = pool_width + 3 * sc_width
    tm_in = min(1024, seq)
    tm_o = min(512, seq)
    assert nb == LANES and seq % tm_in == 0 and seq % GDN_CHUNK == 0

    x_all = jnp.concatenate([x_prompt.reshape(n_prompt, d), x_sample.reshape(nb, d)], axis=0)
    c_all = jnp.concatenate([c_sample, c_prompt, jnp.zeros((8 - batch, d), F32)], axis=0)
    prompt_mod_blk = nb // 8
    mod = ada_mod(c_all, w_ada, b_ada)

    alog_row = _head_row(a_log, 0)
    dtb_row = _head_row(dt_bias, 0)
    w_o_bf = w_o.astype(BF16)
    wr_pad = jnp.zeros((depth, d, LANES), F32).at[:, :, :N_EXPERTS].set(w_router)
    br_pad = jnp.zeros((depth, 1, LANES), F32).at[:, 0, :N_EXPERTS].set(b_router)
    sp_t = jnp.swapaxes(state_pool, 1, 2)
    ssc_t = jnp.swapaxes(state_shortconv, 1, 2)
    sq_t = jnp.swapaxes(state_qkv_conv, 1, 2)

    nb_max = (total * TOP_K + N_EXPERTS * (MOE_BM - 1)) // MOE_BM
    npad = nb_max * MOE_BM
    dff = w_down.shape[2]
    tn_gu, tn_dn = 512, 1024
    w_in_bf = w_in.astype(BF16)

    pools_p, pools_s, scs_p, scs_s, qkvs_p, qkvs_s, deltas_p = ([] for _ in range(7))
    delta_s_all = None
    for l in range(depth):
        proj_p, ab_p = in_proj(x_all, mod, norm1, w_in_bf, l, row0=0, rows=n_prompt, tm=tm_in, tn=1024,
                               per_row=False, rows_per_batch=seq, mod_row_blk=prompt_mod_blk)
        proj_s, ab_s = in_proj(x_all, mod, norm1, w_in_bf, l, row0=n_prompt, rows=nb, tm=nb, tn=1024,
                               per_row=True, rows_per_batch=nb, mod_row_blk=0)
        ya_p = pool_prompt(proj_p, pool_w, pool_scale, l, batch=batch, seq=seq)
        yb_p, sc_new_p = sc_prompt(proj_p, sc_conv_w, l, batch=batch, seq=seq, width=sc_width, col0=col_sc)
        yc_p, s_new_p = gdn_prompt(proj_p, ab_p, qkv_conv_w, alog_row, dtb_row, onorm_w, l,
                                   batch=batch, seq=seq, col0=col_gdn)
        ya_s, yb_s, cx_s, qn, kn, vv, eg, beta = mix_sample(
            proj_s, ab_s, sp_t[l], ssc_t[l], sq_t[l], pool_w, pool_scale, sc_conv_w, qkv_conv_w,
            alog_row, dtb_row, l)
        yc_s, delta_s_all = delta_sample(qn, kn, vv, eg, beta, proj_s, onorm_w, state_delta, l,
                                         zcol_blk=(col_gdn + 3 * gdn_width) // gdn_width, prev=delta_s_all)

        p3 = proj_p.reshape(batch, seq, -1)
        pools_p.append(p3[:, seq - 15:, :pool_width])
        scs_p.append(sc_new_p)
        qkvs_p.append(p3[:, seq - 3:, col_gdn:col_gdn + 3 * gdn_width])
        deltas_p.append(s_new_p)
        pools_s.append(jnp.concatenate([state_pool[l][:, 1:], proj_s[:, None, :pool_width]], axis=1))
        scs_s.append(jnp.concatenate([state_shortconv[l][:, 1:], cx_s[:, None, :]], axis=1))
        qkvs_s.append(jnp.concatenate(
            [state_qkv_conv[l][:, 1:], proj_s[:, None, col_gdn:col_gdn + 3 * gdn_width]], axis=1))

        shared = o_proj(ya_p, yb_p, yc_p, x_all, mod, norm2, w_o_bf, wr_pad, br_pad, l,
                        row0=0, rows=n_prompt, tm=tm_o, per_row=False, rows_per_batch=seq,
                        mod_row_blk=prompt_mod_blk, total_rows=total)
        x1, xu, topi, topp = o_proj(ya_s, yb_s, yc_s, x_all, mod, norm2, w_o_bf, wr_pad, br_pad, l,
                                    row0=n_prompt, rows=nb, tm=nb, per_row=True, rows_per_batch=nb,
                                    mod_row_blk=0, total_rows=total, prev=shared)

        rank, cnt = route_rank(topi)
        counts = cnt[0, :N_EXPERTS]
        (work_gu, work_dn), row_off = make_work_lists(counts, (dff // tn_gu, d // tn_dn), nb_max)
        ti4 = topi[:, :TOP_K]
        pos_flat = (_table_lookup(row_off, ti4) + rank[:, :TOP_K]).reshape(-1).astype(I32)
        xs = dispatch(pos_flat, xu, jnp.zeros((npad, d // 2), U32))
        act = moe_gate_up(work_gu, xs, w_gate_up, b_gate_up, l, tn=tn_gu)
        y_sorted = moe_down(work_dn, act, w_down, b_down, l, tn=tn_dn)
        x_all = combine(pos_flat, y_sorted, topp, x1, mod, l, n_prompt=n_prompt,
                        rows_per_batch=seq, prompt_mod_blk=prompt_mod_blk)

    y_p, y_s = final_norm_split(x_all, final_norm, n_prompt=n_prompt, tb=nb)
    return (y_p.reshape(batch, seq, d), y_s.reshape(nb, 1, d),
            jnp.stack(pools_p), jnp.stack(pools_s), jnp.stack(scs_p), jnp.stack(scs_s),
            jnp.stack(qkvs_p), jnp.stack(qkvs_s), jnp.stack(deltas_p), delta_s_all)
```

```python
import functools

import jax
import jax.numpy as jnp
from jax import lax
from jax.experimental import pallas as pl
from jax.experimental.pallas import tpu as pltpu

F32 = jnp.float32
BF16 = jnp.bfloat16
I32 = jnp.int32
U32 = jnp.uint32

EPS = 1e-6
POOL_WINDOWS = (2, 4, 8, 16)
POOL_GROUP = 128
GDN_HEADS = 8
GDN_DIM = 128
GDN_CHUNK = 64
GDN_STEP_CHUNKS = 2
N_EXPERTS = 32
TOP_K = 4
SWIGLU_LIMIT = 7.0
SWIGLU_ALPHA = 1.702
LANES = 128
MOE_BM = 512
MOE_HALF = MOE_BM // 2
VMEM_LIMIT = 56 * 1024 * 1024


def _cp(sem):
    return pltpu.CompilerParams(dimension_semantics=sem, vmem_limit_bytes=VMEM_LIMIT)


def _silu(x):
    return x * jax.nn.sigmoid(x)


def _softplus(x):
    return jnp.maximum(x, 0.0) + jnp.log1p(jnp.exp(-jnp.abs(x)))


def _bdot(a, b):
    return jnp.dot(a.astype(BF16), b.astype(BF16), preferred_element_type=F32)


def _shift_rows(x, s, row):
    return jnp.where(row >= s, pltpu.roll(x, s, 0), 0.0)


def _ada_kernel(c_ref, w_ref, b_ref, o_ref):
    o_ref[...] = _bdot(_silu(c_ref[...]), w_ref[...]) + b_ref[...]


def ada_mod(c_all, w_ada, b_ada, tn=1024):
    depth, d, n = w_ada.shape
    rows = c_all.shape[0]
    return pl.pallas_call(
        _ada_kernel,
        out_shape=jax.ShapeDtypeStruct((depth, rows, n), F32),
        grid=(depth, n // tn),
        in_specs=[
            pl.BlockSpec((rows, d), lambda l, j: (0, 0)),
            pl.BlockSpec((None, d, tn), lambda l, j: (l, 0, j)),
            pl.BlockSpec((None, 1, tn), lambda l, j: (l, 0, j)),
        ],
        out_specs=pl.BlockSpec((None, rows, tn), lambda l, j: (l, 0, j)),
        compiler_params=_cp(("arbitrary", "arbitrary")),
        name="ada_mod",
    )(c_all, w_ada, b_ada.reshape(depth, 1, n))


def _mod_rows(ref, per_row, brow):
    if per_row:
        return ref[...]
    return ref[pl.ds(brow, 1), :]


def _inproj_kernel(x_ref, sh_ref, sc_ref, n_ref, w_ref, wab_ref, o_ref, oab_ref, h_scr,
                   *, per_row, blocks_per_batch, n_ab):
    i = pl.program_id(0)
    j = pl.program_id(1)

    @pl.when(j == 0)
    def _():
        x = x_ref[...]
        y = x * lax.rsqrt(jnp.mean(x * x, -1, keepdims=True) + EPS) * n_ref[...]
        brow = i // blocks_per_batch
        h = y * (1.0 + _mod_rows(sc_ref, per_row, brow)) + _mod_rows(sh_ref, per_row, brow)
        hb = h.astype(BF16)
        h_scr[...] = hb
        col = lax.broadcasted_iota(I32, wab_ref.shape, 1)
        wab = jnp.where(col < n_ab, wab_ref[...], 0.0)
        oab_ref[...] = _bdot(hb, wab)

    o_ref[...] = jnp.dot(h_scr[...], w_ref[...], preferred_element_type=F32)


def in_proj(x_all, mod, norm_w, w_in, layer, *, row0, rows, tm, tn, per_row, rows_per_batch, mod_row_blk):
    depth, d, n_cols = w_in.shape
    n_main = (n_cols // LANES) * LANES
    n_ab = n_cols - n_main
    mrows = tm if per_row else 8
    mod_spec = lambda chunk: pl.BlockSpec((None, mrows, d), lambda i, j: (layer, mod_row_blk, chunk))
    kern = functools.partial(_inproj_kernel, per_row=per_row,
                             blocks_per_batch=max(rows_per_batch // tm, 1), n_ab=n_ab)
    return pl.pallas_call(
        kern,
        out_shape=(jax.ShapeDtypeStruct((rows, n_main), F32), jax.ShapeDtypeStruct((rows, LANES), F32)),
        grid=(rows // tm, n_main // tn),
        in_specs=[
            pl.BlockSpec((tm, d), lambda i, j: (row0 // tm + i, 0)),
            mod_spec(0), mod_spec(1),
            pl.BlockSpec((None, 1, d), lambda i, j: (layer, 0, 0)),
            pl.BlockSpec((None, d, tn), lambda i, j: (layer, 0, j)),
            pl.BlockSpec((None, d, LANES), lambda i, j: (layer, 0, n_main // LANES)),
        ],
        out_specs=(pl.BlockSpec((tm, tn), lambda i, j: (i, j)),
                   pl.BlockSpec((tm, LANES), lambda i, j: (i, 0))),
        scratch_shapes=[pltpu.VMEM((tm, d), BF16)],
        compiler_params=_cp(("arbitrary", "arbitrary")),
        name="in_proj",
    )(x_all, mod, mod, norm_w.reshape(depth, 1, d), w_in, w_in)


def _pool_p_kernel(u_ref, w_ref, s_ref, o_ref):
    g = pl.program_id(1)
    u = u_ref[...]
    row = lax.broadcasted_iota(I32, u.shape, 0)
    w2 = u + _shift_rows(u, 1, row)
    w4 = w2 + _shift_rows(w2, 2, row)
    w8 = w4 + _shift_rows(w4, 4, row)
    w16 = w8 + _shift_rows(w8, 8, row)
    wsum = jnp.where(g == 0, w2, jnp.where(g == 1, w4, jnp.where(g == 2, w8, w16)))
    win = jnp.left_shift(2, g)
    cnt = jnp.minimum(row + 1, win).astype(F32)
    d = wsum / cnt - u
    o_ref[...] = (_bdot(d, w_ref[...]) * s_ref[...]).astype(o_ref.dtype)


def pool_prompt(proj, pool_w, pool_scale, layer, *, batch, seq):
    depth = pool_w.shape[0]
    ng = len(POOL_WINDOWS)
    return pl.pallas_call(
        _pool_p_kernel,
        out_shape=jax.ShapeDtypeStruct((batch * seq, ng * POOL_GROUP), BF16),
        grid=(batch, ng),
        in_specs=[
            pl.BlockSpec((seq, POOL_GROUP), lambda b, g: (b, g)),
            pl.BlockSpec((None, None, POOL_GROUP, POOL_GROUP), lambda b, g: (layer, g, 0, 0)),
            pl.BlockSpec((None, 1, POOL_GROUP), lambda b, g: (layer, 0, g)),
        ],
        out_specs=pl.BlockSpec((seq, POOL_GROUP), lambda b, g: (b, g)),
        compiler_params=_cp(("arbitrary", "arbitrary")),
        name="pool_prompt",
    )(proj, pool_w, pool_scale.reshape(depth, 1, ng * POOL_GROUP))


def _sc_p_kernel(x_ref, b_ref, c_ref, w_ref, o_ref, ns_ref):
    cx = c_ref[...] * x_ref[...]
    row = lax.broadcasted_iota(I32, cx.shape, 0)
    w = w_ref[...]
    y = _shift_rows(cx, 2, row) * w[0:1] + _shift_rows(cx, 1, row) * w[1:2] + cx * w[2:3]
    o_ref[...] = (b_ref[...] * y).astype(o_ref.dtype)
    n = cx.shape[0]
    ns_ref[...] = cx[n - 2:n, :]


def sc_prompt(proj, sc_conv_w, layer, *, batch, seq, width, col0, tc=256):
    nct = width // tc
    cb = col0 // tc
    return pl.pallas_call(
        _sc_p_kernel,
        out_shape=(jax.ShapeDtypeStruct((batch * seq, width), BF16),
                   jax.ShapeDtypeStruct((batch, 2, width), F32)),
        grid=(batch, nct),
        in_specs=[
            pl.BlockSpec((seq, tc), lambda b, c: (b, cb + c)),
            pl.BlockSpec((seq, tc), lambda b, c: (b, cb + nct + c)),
            pl.BlockSpec((seq, tc), lambda b, c: (b, cb + 2 * nct + c)),
            pl.BlockSpec((None, 3, tc), lambda b, c: (layer, 0, c)),
        ],
        out_specs=(pl.BlockSpec((seq, tc), lambda b, c: (b, c)),
                   pl.BlockSpec((None, 2, tc), lambda b, c: (b, 0, c))),
        compiler_params=_cp(("arbitrary", "arbitrary")),
        name="sc_prompt",
    )(proj, proj, proj, sc_conv_w)


def _inv_unit_lower_minus_eye(lms):
    c = lms[0].shape[0]
    ps = [-lm for lm in lms]
    rs = list(ps)
    steps = max(c.bit_length() - 2, 0)
    for _ in range(steps):
        ps = [_bdot(p, p) for p in ps]
        rs = [r + p + _bdot(r, p) for r, p in zip(rs, ps)]
    return rs


def _gdn_p_kernel(qc_ref, kc_ref, vc_ref, qp_ref, kp_ref, vp_ref, z_ref, ab_ref,
                  cwq_ref, cwk_ref, cwv_ref, alog_ref, dtb_ref, on_ref,
                  o_ref, sout_ref, s_scr):
    n = pl.program_id(1)
    rows = qc_ref.shape[0]
    c = GDN_CHUNK
    hd = GDN_DIM

    @pl.when(n == 0)
    def _():
        s_scr[...] = jnp.zeros(s_scr.shape, F32)

    has_prev = n > 0

    def conv_silu(cur_ref, prev_ref, w_ref):
        prev = jnp.where(has_prev, prev_ref[...], 0.0)
        xx = jnp.concatenate([prev, cur_ref[...]], axis=0)
        w = w_ref[...]
        y = (xx[5:5 + rows] * w[0:1] + xx[6:6 + rows] * w[1:2]
             + xx[7:7 + rows] * w[2:3] + xx[8:8 + rows] * w[3:4])
        return _silu(y)

    q = conv_silu(qc_ref, qp_ref, cwq_ref)
    k = conv_silu(kc_ref, kp_ref, cwk_ref)
    v = conv_silu(vc_ref, vp_ref, cwv_ref)
    z = z_ref[...]

    ab = ab_ref[...]
    g_all = -jnp.exp(alog_ref[...]) * _softplus(ab + dtb_ref[...])
    beta_all = jax.nn.sigmoid(ab)
    row_in_chunk = lax.broadcasted_iota(I32, g_all.shape, 0) % c
    gc = g_all
    s = 1
    while s < c:
        gc = gc + _shift_rows(gc, s, row_in_chunk)
        s *= 2
    eg_all = jnp.exp(gc)

    ri = lax.broadcasted_iota(I32, (c, c), 0)
    ci = lax.broadcasted_iota(I32, (c, c), 1)
    incl = ri >= ci
    strict = ri > ci
    on = on_ref[...]

    heads = range(GDN_HEADS)
    chunks = range(rows // c)
    nt = (((1,), (1,)), ((), ()))
    tn_dims = (((0,), (0,)), ((), ()))
    qns, kns, rhss, lms, aintras = [], [], [], [], []
    for ch in chunks:
        rs_ = slice(ch * c, (ch + 1) * c)
        gct = gc[rs_].T
        for h in heads:
            sl = slice(h * hd, (h + 1) * hd)
            qh, kh, vh = q[rs_, sl], k[rs_, sl], v[rs_, sl]
            qn = qh * lax.rsqrt(jnp.sum(qh * qh, -1, keepdims=True) + EPS) * (hd ** -0.5)
            kn = kh * lax.rsqrt(jnp.sum(kh * kh, -1, keepdims=True) + EPS)
            gcol = gc[rs_, h:h + 1]
            grow = gct[h:h + 1, :]
            bcol = beta_all[rs_, GDN_HEADS + h:GDN_HEADS + h + 1]
            decay = jnp.where(incl, jnp.exp(jnp.minimum(gcol - grow, 0.0)), 0.0)
            kb = kn * bcol
            knb = kn.astype(BF16)
            kk = lax.dot_general(kb.astype(BF16), knb, nt, preferred_element_type=F32)
            qk = lax.dot_general(qn.astype(BF16), knb, nt, preferred_element_type=F32)
            lms.append(jnp.where(strict, kk * decay, 0.0))
            aintras.append(jnp.where(incl, qk * decay, 0.0))
            rhss.append(jnp.concatenate([vh * bcol, kb * eg_all[rs_, h:h + 1]], axis=-1))
            qns.append(qn)
            kns.append(kn)
    rs = _inv_unit_lower_minus_eye(lms)
    sols = [rhs + _bdot(r, rhs) for r, rhs in zip(rs, rhss)]
    for ch in chunks:
        rs_ = slice(ch * c, (ch + 1) * c)
        for h in heads:
            i = ch * GDN_HEADS + h
            sl = slice(h * hd, (h + 1) * hd)
            gcol = gc[rs_, h:h + 1]
            u_val, k_cum = sols[i][:, :hd], sols[i][:, hd:]
            s_old = s_scr[h]
            v_new = u_val - _bdot(k_cum, s_old)
            o = _bdot(qns[i] * eg_all[rs_, h:h + 1], s_old) + _bdot(aintras[i], v_new)
            glast = gc[(ch + 1) * c - 1:(ch + 1) * c, h:h + 1]
            kd = kns[i] * jnp.exp(glast - gcol)
            s_scr[h] = s_old * jnp.exp(glast) + lax.dot_general(
                kd.astype(BF16), v_new.astype(BF16), tn_dims, preferred_element_type=F32)
            o = o * lax.rsqrt(jnp.mean(o * o, -1, keepdims=True) + EPS) * on
            o_ref[rs_, sl] = (o * _silu(z[rs_, sl])).astype(o_ref.dtype)

    @pl.when(n == pl.num_programs(1) - 1)
    def _():
        sout_ref[...] = s_scr[...]


def _head_row(v, offset):
    depth, h = v.shape
    return jnp.zeros((depth, 1, LANES), F32).at[:, 0, offset:offset + h].set(v.astype(F32))


def gdn_prompt(proj, ab, qkv_conv_w, alog_row, dtb_row, onorm_w, layer, *, batch, seq, col0):
    depth = qkv_conv_w.shape[0]
    wdt = GDN_HEADS * GDN_DIM
    c = GDN_CHUNK * GDN_STEP_CHUNKS
    nchunk = seq // c
    cb = col0 // wdt
    cur = lambda t: pl.BlockSpec((c, wdt), lambda b, n: (b * nchunk + n, cb + t))
    prev = lambda t: pl.BlockSpec(
        (8, wdt), lambda b, n: (jnp.maximum((b * nchunk + n) * (c // 8) - 1, 0), cb + t))
    cw = lambda t: pl.BlockSpec((None, 4, wdt), lambda b, n: (layer, 0, t))
    row = pl.BlockSpec((None, 1, LANES), lambda b, n: (layer, 0, 0))
    return pl.pallas_call(
        _gdn_p_kernel,
        out_shape=(jax.ShapeDtypeStruct((batch * seq, wdt), BF16),
                   jax.ShapeDtypeStruct((batch, GDN_HEADS, GDN_DIM, GDN_DIM), F32)),
        grid=(batch, nchunk),
        in_specs=[cur(0), cur(1), cur(2), prev(0), prev(1), prev(2), cur(3),
                  pl.BlockSpec((c, LANES), lambda b, n: (b * nchunk + n, 0)),
                  cw(0), cw(1), cw(2), row, row, row],
        out_specs=(pl.BlockSpec((c, wdt), lambda b, n: (b * nchunk + n, 0)),
                   pl.BlockSpec((None, GDN_HEADS, GDN_DIM, GDN_DIM), lambda b, n: (b, 0, 0, 0))),
        scratch_shapes=[pltpu.VMEM((GDN_HEADS, GDN_DIM, GDN_DIM), F32)],
        compiler_params=_cp(("arbitrary", "arbitrary")),
        name="gdn_prompt",
    )(proj, proj, proj, proj, proj, proj, proj, ab,
      qkv_conv_w, qkv_conv_w, qkv_conv_w, alog_row, dtb_row,
      onorm_w.reshape(depth, 1, GDN_DIM))


def _mix_s_kernel(proj_ref, ab_ref, sp_ref, ssc_ref, sq_ref, pw_ref, ps_ref, scw_ref, cw_ref,
                  alog_ref, dtb_ref,
                  ya_ref, yb_ref, cx_ref, q_ref, k_ref, v_ref, eg_ref, beta_ref,
                  *, pool_w, sc_w, gdn_w):
    for g, win in enumerate(POOL_WINDOWS):
        sl = slice(g * POOL_GROUP, (g + 1) * POOL_GROUP)
        u = proj_ref[:, sl]
        acc = u
        nbuf = sp_ref.shape[0]
        for r in range(nbuf - (win - 1), nbuf):
            acc = acc + sp_ref[r, :, sl]
        d = acc / float(win) - u
        ya_ref[:, sl] = (_bdot(d, pw_ref[g]) * ps_ref[:, sl]).astype(ya_ref.dtype)
    c0 = pool_w
    xs = proj_ref[:, c0:c0 + sc_w]
    bg = proj_ref[:, c0 + sc_w:c0 + 2 * sc_w]
    cg = proj_ref[:, c0 + 2 * sc_w:c0 + 3 * sc_w]
    cx = cg * xs
    w = scw_ref[...]
    y = ssc_ref[0] * w[0:1] + ssc_ref[1] * w[1:2] + cx * w[2:3]
    yb_ref[...] = (bg * y).astype(yb_ref.dtype)
    cx_ref[...] = cx
    c1 = c0 + 3 * sc_w
    hd = GDN_DIM
    outs = (q_ref, k_ref, v_ref)
    for t in range(3):
        new = proj_ref[:, c1 + t * gdn_w:c1 + (t + 1) * gdn_w]
        ws = slice(t * gdn_w, (t + 1) * gdn_w)
        y = (sq_ref[0, :, ws] * cw_ref[0:1, ws] + sq_ref[1, :, ws] * cw_ref[1:2, ws]
             + sq_ref[2, :, ws] * cw_ref[2:3, ws] + new * cw_ref[3:4, ws])
        y = _silu(y)
        for h in range(GDN_HEADS):
            sl = slice(h * hd, (h + 1) * hd)
            yh = y[:, sl]
            if t == 0:
                yh = yh * lax.rsqrt(jnp.sum(yh * yh, -1, keepdims=True) + EPS) * (hd ** -0.5)
            elif t == 1:
                yh = yh * lax.rsqrt(jnp.sum(yh * yh, -1, keepdims=True) + EPS)
            outs[t][:, sl] = yh
    ab = ab_ref[...]
    eg_ref[...] = jnp.exp(-jnp.exp(alog_ref[...]) * _softplus(ab + dtb_ref[...]))
    beta_ref[...] = jax.nn.sigmoid(ab)


def mix_sample(proj, ab, sp_t, ssc_t, sq_t, pool_w, pool_scale, sc_conv_w, qkv_conv_w,
               alog_row, dtb_row, layer):
    nb = proj.shape[0]
    depth = pool_w.shape[0]
    pw = pool_scale.shape[1]
    scw = sc_conv_w.shape[2]
    gw = qkv_conv_w.shape[2] // 3
    full = lambda a: pl.BlockSpec(a.shape, lambda i: (0,) * a.ndim)
    lay = lambda a: pl.BlockSpec((None,) + a.shape[1:], lambda i: (layer,) + (0,) * (a.ndim - 1))
    ps3 = pool_scale.reshape(depth, 1, pw)
    kern = functools.partial(_mix_s_kernel, pool_w=pw, sc_w=scw, gdn_w=gw)
    shp = lambda w, dt: jax.ShapeDtypeStruct((nb, w), dt)
    outs = (shp(pw, BF16), shp(scw, BF16), shp(scw, F32), shp(gw, F32), shp(gw, F32), shp(gw, F32),
            shp(LANES, F32), shp(LANES, F32))
    return pl.pallas_call(
        kern,
        out_shape=outs,
        grid=(1,),
        in_specs=[full(proj), full(ab), full(sp_t), full(ssc_t), full(sq_t),
                  lay(pool_w), lay(ps3), lay(sc_conv_w), lay(qkv_conv_w), lay(alog_row), lay(dtb_row)],
        out_specs=tuple(pl.BlockSpec(o.shape, lambda i: (0, 0)) for o in outs),
        compiler_params=_cp(("arbitrary",)),
        name="mix_sample",
    )(proj, ab, sp_t, ssc_t, sq_t, pool_w, ps3, sc_conv_w, qkv_conv_w, alog_row, dtb_row)


def _delta_s_kernel(q_ref, k_ref, v_ref, eg_ref, beta_ref, z_ref, on_ref, s_ref, *rest, bb):
    o_ref, so_ref = rest[-2:]
    i = pl.program_id(0)
    hd = GDN_DIM
    on = on_ref[...]
    tn_dims = (((0,), (0,)), ((), ()))
    heads = range(GDN_HEADS)

    def body(r, carry):
        t = i * bb + r
        qrow = q_ref[pl.ds(t, 1), :]
        krow = k_ref[pl.ds(t, 1), :]
        vrow = v_ref[pl.ds(t, 1), :]
        egrow = eg_ref[pl.ds(t, 1), :]
        brow = beta_ref[pl.ds(t, 1), :]
        zrow = z_ref[pl.ds(t, 1), :]
        ress = []
        for h in heads:
            sl = slice(h * hd, (h + 1) * hd)
            kq = jnp.concatenate([krow[:, sl], qrow[:, sl], jnp.zeros((6, hd), F32)], axis=0)
            ress.append(_bdot(kq, s_ref[r, h]))
        upds = []
        for h in heads:
            sl = slice(h * hd, (h + 1) * hd)
            qh, kh, vh = qrow[:, sl], krow[:, sl], vrow[:, sl]
            eg = egrow[:, h:h + 1]
            bt = brow[:, GDN_HEADS + h:GDN_HEADS + h + 1]
            delta = (vh - eg * ress[h][0:1]) * bt
            qk = jnp.sum(qh * kh, -1, keepdims=True)
            o = eg * ress[h][1:2] + qk * delta
            o = o * lax.rsqrt(jnp.mean(o * o, -1, keepdims=True) + EPS) * on
            o_ref[r, :, sl] = o * _silu(zrow[:, sl])
            khi = kh.astype(BF16).astype(F32)
            dhi = delta.astype(BF16).astype(F32)
            zpad = jnp.zeros((13, hd), F32)
            lhs = jnp.concatenate([khi, kh - khi, khi, zpad], axis=0).astype(BF16)
            rhs = jnp.concatenate([dhi, dhi, delta - dhi, zpad], axis=0).astype(BF16)
            upds.append(lax.dot_general(lhs, rhs, tn_dims, preferred_element_type=F32))
        for h in heads:
            so_ref[r, h] = s_ref[r, h] * egrow[:, h:h + 1] + upds[h]
        return carry

    lax.fori_loop(0, bb, body, 0)


def delta_sample(qn, kn, vv, eg, beta, proj, onorm_w, state_delta, layer, *, zcol_blk, prev=None, bb=8):
    nb, gw = qn.shape
    depth = onorm_w.shape[0]
    full = lambda a: pl.BlockSpec(a.shape, lambda i: (0, 0))
    sblk = (None, bb, GDN_HEADS, GDN_DIM, GDN_DIM)
    in_specs = [full(qn), full(kn), full(vv), full(eg), full(beta),
                pl.BlockSpec((nb, gw), lambda i: (0, zcol_blk)),
                pl.BlockSpec((None, 1, GDN_DIM), lambda i: (layer, 0, 0)),
                pl.BlockSpec(sblk, lambda i: (layer, i, 0, 0, 0))]
    args = [qn, kn, vv, eg, beta, proj, onorm_w.reshape(depth, 1, GDN_DIM), state_delta]
    aliases = {}
    if prev is not None:
        in_specs.append(pl.BlockSpec(memory_space=pl.ANY))
        args.append(prev)
        aliases = {len(args) - 1: 1}
    o, s_new = pl.pallas_call(
        functools.partial(_delta_s_kernel, bb=bb),
        out_shape=(jax.ShapeDtypeStruct((nb, 1, gw), F32),
                   jax.ShapeDtypeStruct(state_delta.shape, F32)),
        grid=(nb // bb,),
        in_specs=in_specs,
        out_specs=(pl.BlockSpec((bb, 1, gw), lambda i: (i, 0, 0)),
                   pl.BlockSpec(sblk, lambda i: (layer, i, 0, 0, 0))),
        input_output_aliases=aliases,
        compiler_params=_cp(("arbitrary",)),
        name="delta_sample",
    )(*args)
    return o.reshape(nb, gw), s_new


def _pack_halves(h):
    half = h.shape[1] // 2
    lo = pltpu.bitcast(h[:, :half].astype(BF16).astype(F32), U32)
    hi = pltpu.bitcast(h[:, half:].astype(BF16).astype(F32), U32)
    return (lo >> 16) | (hi & jnp.uint32(0xFFFF0000))


def _unpack_halves(xu):
    lo = pltpu.bitcast(xu << 16, F32).astype(BF16)
    hi = pltpu.bitcast(xu & jnp.uint32(0xFFFF0000), F32).astype(BF16)
    return lo, hi


def _oproj_kernel(*refs, per_row, blocks_per_batch, n_alias):
    (ya_ref, yb_ref, yc_ref, x_ref, g1_ref, sh_ref, sc_ref, n_ref,
     woa_ref, wob_ref, woc_ref, wr_ref, br_ref) = refs[:13]
    x1_ref, xu_ref, ti_ref, tp_ref = refs[13 + n_alias:13 + n_alias + 4]
    i = pl.program_id(0)
    brow = i // blocks_per_batch
    acc = (jnp.dot(ya_ref[...].astype(BF16), woa_ref[...], preferred_element_type=F32)
           + jnp.dot(yb_ref[...].astype(BF16), wob_ref[...], preferred_element_type=F32)
           + jnp.dot(yc_ref[...].astype(BF16), woc_ref[...], preferred_element_type=F32))
    x1 = x_ref[...] + _mod_rows(g1_ref, per_row, brow) * acc
    x1_ref[...] = x1
    y = x1 * lax.rsqrt(jnp.mean(x1 * x1, -1, keepdims=True) + EPS) * n_ref[...]
    h = y * (1.0 + _mod_rows(sc_ref, per_row, brow)) + _mod_rows(sh_ref, per_row, brow)
    xu_ref[...] = _pack_halves(h)
    hhi = h.astype(BF16)
    hlo = (h - hhi.astype(F32)).astype(BF16)
    wr = wr_ref[...]
    whi = wr.astype(BF16)
    wlo = (wr - whi.astype(F32)).astype(BF16)
    logits = (jnp.dot(hhi, whi, preferred_element_type=F32)
              + jnp.dot(hlo, whi, preferred_element_type=F32)
              + jnp.dot(hhi, wlo, preferred_element_type=F32)) + br_ref[...]
    lane = lax.broadcasted_iota(I32, logits.shape, 1)
    neg = jnp.float32(-jnp.inf)
    cur = jnp.where(lane < N_EXPERTS, logits, neg)
    ti = jnp.zeros(logits.shape, I32)
    tv = jnp.full(logits.shape, neg, F32)
    for kk in range(TOP_K):
        m = jnp.max(cur, -1, keepdims=True)
        idx = jnp.min(jnp.where(cur == m, lane, LANES), -1, keepdims=True)
        ti = jnp.where(lane == kk, idx, ti)
        tv = jnp.where(lane == kk, m, tv)
        cur = jnp.where(lane == idx, neg, cur)
    e = jnp.exp(tv - jnp.max(tv, -1, keepdims=True))
    ti_ref[...] = ti
    tp_ref[...] = e / jnp.sum(e, -1, keepdims=True)


def o_proj(ya, yb, yc, x_all, mod, norm_w, w_o_bf, w_router_pad, b_router_pad, layer, *,
           row0, rows, tm, per_row, rows_per_batch, mod_row_blk, total_rows, prev=None):
    depth, d, _ = w_o_bf.shape
    wa, wb, wc = ya.shape[1], yb.shape[1], yc.shape[1]
    mrows = tm if per_row else 8
    rb = row0 // tm
    mod_spec = lambda chunk: pl.BlockSpec((None, mrows, d), lambda i: (layer, mod_row_blk, chunk))
    n_alias = 0 if prev is None else 4
    kern = functools.partial(_oproj_kernel, per_row=per_row,
                             blocks_per_batch=max(rows_per_batch // tm, 1), n_alias=n_alias)
    in_specs = [
        pl.BlockSpec((tm, wa), lambda i: (i, 0)),
        pl.BlockSpec((tm, wb), lambda i: (i, 0)),
        pl.BlockSpec((tm, wc), lambda i: (i, 0)),
        pl.BlockSpec((tm, d), lambda i: (rb + i, 0)),
        mod_spec(2), mod_spec(3), mod_spec(4),
        pl.BlockSpec((None, 1, d), lambda i: (layer, 0, 0)),
        pl.BlockSpec((None, wa, d), lambda i: (layer, 0, 0)),
        pl.BlockSpec((None, wb, d), lambda i: (layer, wa // wb, 0)),
        pl.BlockSpec((None, wc, d), lambda i: (layer, (wa + wb) // wc, 0)),
        pl.BlockSpec((None, d, LANES), lambda i: (layer, 0, 0)),
        pl.BlockSpec((None, 1, LANES), lambda i: (layer, 0, 0)),
    ]
    args = [ya, yb, yc, x_all, mod, mod, mod, norm_w.reshape(depth, 1, d),
            w_o_bf, w_o_bf, w_o_bf, w_router_pad, b_router_pad]
    aliases = {}
    if prev is not None:
        in_specs += [pl.BlockSpec(memory_space=pl.ANY)] * 4
        args += list(prev)
        aliases = {13 + t: t for t in range(4)}
    out_shape = (jax.ShapeDtypeStruct((total_rows, d), F32),
                 jax.ShapeDtypeStruct((total_rows, d // 2), U32),
                 jax.ShapeDtypeStruct((total_rows, LANES), I32),
                 jax.ShapeDtypeStruct((total_rows, LANES), F32))
    out_specs = (pl.BlockSpec((tm, d), lambda i: (rb + i, 0)),
                 pl.BlockSpec((tm, d // 2), lambda i: (rb + i, 0)),
                 pl.BlockSpec((tm, LANES), lambda i: (rb + i, 0)),
                 pl.BlockSpec((tm, LANES), lambda i: (rb + i, 0)))
    return pl.pallas_call(
        kern, out_shape=out_shape, grid=(rows // tm,),
        in_specs=in_specs, out_specs=out_specs,
        input_output_aliases=aliases,
        compiler_params=_cp(("arbitrary",)),
        name="o_proj",
    )(*args)


def _rank_kernel(ti_ref, rank_ref, cnt_ref, run_scr):
    i = pl.program_id(0)

    @pl.when(i == 0)
    def _():
        run_scr[...] = jnp.zeros(run_scr.shape, F32)

    ti = ti_ref[...]
    tb = ti.shape[0]
    lane = lax.broadcasted_iota(I32, ti.shape, 1)
    sel = jnp.zeros(ti.shape, F32)
    for kk in range(TOP_K):
        sel = sel + jnp.where(lane == ti[:, kk:kk + 1], 1.0, 0.0)
    r = lax.broadcasted_iota(I32, (tb, tb), 0)
    q = lax.broadcasted_iota(I32, (tb, tb), 1)
    tri = jnp.where(r > q, 1.0, 0.0)
    before = _bdot(tri, sel) + run_scr[0:1, :]
    rank = jnp.zeros(ti.shape, F32)
    for kk in range(TOP_K):
        rk = jnp.sum(jnp.where(lane == ti[:, kk:kk + 1], before, 0.0), -1, keepdims=True)
        rank = jnp.where(lane == kk, rk, rank)
    rank_ref[...] = rank.astype(I32)
    total = run_scr[0:1, :] + jnp.sum(sel, 0, keepdims=True)
    run_scr[...] = jnp.broadcast_to(total, run_scr.shape)
    cnt_ref[...] = jnp.broadcast_to(total, cnt_ref.shape).astype(I32)


def route_rank(topi, tb=128):
    t = topi.shape[0]
    return pl.pallas_call(
        _rank_kernel,
        out_shape=(jax.ShapeDtypeStruct((t, LANES), I32), jax.ShapeDtypeStruct((8, LANES), I32)),
        grid=(t // tb,),
        in_specs=[pl.BlockSpec((tb, LANES), lambda i: (i, 0))],
        out_specs=(pl.BlockSpec((tb, LANES), lambda i: (i, 0)),
                   pl.BlockSpec((8, LANES), lambda i: (0, 0))),
        scratch_shapes=[pltpu.VMEM((8, LANES), F32)],
        compiler_params=_cp(("arbitrary",)),
        name="route_rank",
    )(topi)


def _dispatch_kernel(pos_ref, xu_ref, xs_in_ref, xs_ref, sem, *, tb):
    del xs_in_ref
    i = pl.program_id(0)
    base = i * tb * TOP_K

    def row_copy(r, kk):
        p = pos_ref[base + r * TOP_K + kk]
        return pltpu.make_async_copy(xu_ref.at[pl.ds(r, 1)], xs_ref.at[pl.ds(p, 1)], sem)

    def wait(r, c):
        for kk in range(TOP_K):
            row_copy(r, kk).wait()
        return c

    for r in range(tb):
        for kk in range(TOP_K):
            row_copy(r, kk).start()
    lax.fori_loop(0, tb, wait, 0, unroll=4)


def _zero_tail_kernel(blk_ref, o_ref):
    del blk_ref
    o_ref[...] = jnp.zeros(o_ref.shape, o_ref.dtype)


def zero_tail_blocks(tail_blk, npad, width):
    return pl.pallas_call(
        _zero_tail_kernel,
        out_shape=jax.ShapeDtypeStruct((npad, width), U32),
        grid_spec=pltpu.PrefetchScalarGridSpec(
            num_scalar_prefetch=1,
            grid=(tail_blk.shape[0],),
            in_specs=[],
            out_specs=pl.BlockSpec((MOE_HALF, width), lambda e, blk: (blk[e], 0)),
        ),
        compiler_params=_cp(("arbitrary",)),
        name="moe_zero_tail",
    )(tail_blk)


def dispatch(pos_flat, xu, xs_zero, tb=128):
    t, w = xu.shape
    return pl.pallas_call(
        functools.partial(_dispatch_kernel, tb=tb),
        out_shape=jax.ShapeDtypeStruct(xs_zero.shape, xs_zero.dtype),
        grid_spec=pltpu.PrefetchScalarGridSpec(
            num_scalar_prefetch=1,
            grid=(t // tb,),
            in_specs=[pl.BlockSpec((tb, w), lambda i, pos: (i, 0)),
                      pl.BlockSpec(memory_space=pl.ANY)],
            out_specs=pl.BlockSpec(memory_space=pl.ANY),
            scratch_shapes=[pltpu.SemaphoreType.DMA(())],
        ),
        input_output_aliases={2: 0},
        compiler_params=_cp(("arbitrary",)),
        name="moe_dispatch",
    )(pos_flat, xu, xs_zero)


FLAG_VALID, FLAG_FIRST, FLAG_HALF, FLAG_SLOT, FLAG_NEXT = 1, 2, 4, 8, 16


def _expert_mm_kernel(we_ref, wj_ref, wi_ref, wfl_ref, wne_ref, wnj_ref, *refs, layer, tn, up_col0):
    gate_up = up_col0 is not None
    if gate_up:
        x_ref, w_hbm, bg_ref, bu_ref, o_ref, wf32, wbf, sem = refs
    else:
        x_ref, w_hbm, bd_ref, o_ref, wf32, wbf, sem = refs
    ntile = 2 if gate_up else 1
    w = pl.program_id(0)
    fl = wfl_ref[w]
    has = lambda bit: (fl & bit) != 0
    slot = jnp.where(has(FLAG_SLOT), 1, 0)

    def wcopy(e, j, sl, t):
        col = pl.multiple_of(j * tn + (up_col0 if t else 0), tn)
        return pltpu.make_async_copy(w_hbm.at[layer, e, :, pl.ds(col, tn)], wf32.at[sl, t], sem.at[sl, t])

    @pl.when(w == 0)
    def _():
        for t in range(ntile):
            wcopy(we_ref[0], wj_ref[0], 0, t).start()

    @pl.when(has(FLAG_FIRST))
    def _():
        for t in range(ntile):
            wcopy(we_ref[w], wj_ref[w], slot, t).wait()

        @pl.when(has(FLAG_NEXT))
        def _():
            for t in range(ntile):
                wcopy(wne_ref[w], wnj_ref[w], 1 - slot, t).start()

        for t in range(ntile):
            wbf[t] = wf32[slot, t].astype(BF16)

    def compute(m):
        if gate_up:
            lo, hi = _unpack_halves(x_ref[0:m, :])
            half = lo.shape[1]
            gate = (jnp.dot(lo, wbf[0, :half], preferred_element_type=F32)
                    + jnp.dot(hi, wbf[0, half:], preferred_element_type=F32) + bg_ref[...])
            up = (jnp.dot(lo, wbf[1, :half], preferred_element_type=F32)
                  + jnp.dot(hi, wbf[1, half:], preferred_element_type=F32) + bu_ref[...])
            gate = jnp.minimum(gate, SWIGLU_LIMIT)
            up = jnp.clip(up, -SWIGLU_LIMIT, SWIGLU_LIMIT)
            o_ref[0:m, :] = ((up + 1.0) * gate * jax.nn.sigmoid(SWIGLU_ALPHA * gate)).astype(o_ref.dtype)
        else:
            o_ref[0:m, :] = jnp.dot(x_ref[0:m, :], wbf[0], preferred_element_type=F32) + bd_ref[...]

    @pl.when(has(FLAG_VALID) & jnp.logical_not(has(FLAG_HALF)))
    def _():
        compute(MOE_BM)

    @pl.when(has(FLAG_VALID) & has(FLAG_HALF))
    def _():
        compute(MOE_HALF)


def _expert_mm(work, x, w, bias, layer, *, tn, gate_up, out_dtype):
    depth, ne, k, n_all = w.shape
    n_out = n_all // 2 if gate_up else n_all
    nj = n_out // tn
    npad = x.shape[0]
    nwork = work[0].shape[0]
    b4 = bias.reshape(depth, ne, 1, n_all)
    ntile = 2 if gate_up else 1
    x_spec = pl.BlockSpec((MOE_BM, x.shape[1]), lambda w_, we, wj, wi, *_: (wi[w_], 0))
    b_spec = lambda off: pl.BlockSpec((None, None, 1, tn), lambda w_, we, wj, *_: (layer, we[w_], 0, off + wj[w_]))
    in_specs = [x_spec, pl.BlockSpec(memory_space=pl.ANY), b_spec(0)] + ([b_spec(nj)] if gate_up else [])
    args = [x, w, b4] + ([b4] if gate_up else [])
    kern = functools.partial(_expert_mm_kernel, layer=layer, tn=tn, up_col0=n_out if gate_up else None)
    return pl.pallas_call(
        kern,
        out_shape=jax.ShapeDtypeStruct((npad, n_out), out_dtype),
        grid_spec=pltpu.PrefetchScalarGridSpec(
            num_scalar_prefetch=6,
            grid=(nwork,),
            in_specs=in_specs,
            out_specs=pl.BlockSpec((MOE_BM, tn), lambda w_, we, wj, wi, *_: (wi[w_], wj[w_])),
            scratch_shapes=[pltpu.VMEM((2, ntile, k, tn), F32), pltpu.VMEM((ntile, k, tn), BF16),
                            pltpu.SemaphoreType.DMA((2, ntile))],
        ),
        compiler_params=_cp(("arbitrary",)),
        name="moe_gate_up" if gate_up else "moe_down",
    )(*work, *args)


def moe_gate_up(work, xs, w_gate_up, b_gate_up, layer, *, tn):
    return _expert_mm(work, xs, w_gate_up, b_gate_up, layer, tn=tn, gate_up=True, out_dtype=BF16)


def moe_down(work, act, w_down, b_down, layer, *, tn):
    return _expert_mm(work, act, w_down, b_down, layer, tn=tn, gate_up=False, out_dtype=F32)


def make_work_lists(counts, njs, nb_max):
    n_half = (counts + MOE_HALF - 1) // MOE_HALF
    nb = (n_half + 1) // 2
    odd = n_half % 2
    blk_off = jnp.cumsum(nb) - nb
    total = jnp.sum(nb)
    nonempty = (nb > 0).astype(I32)
    groups_before = jnp.cumsum(nonempty) - nonempty
    ar = jnp.arange(N_EXPERTS, dtype=I32)
    later = (ar[None, :] > ar[:, None]) & (nb[None, :] > 0)
    next_e = jnp.min(jnp.where(later, ar[None, :], N_EXPERTS), axis=1)
    lists = []
    for nj in njs:
        start = nj * blk_off
        nvalid = nj * total
        w = jnp.arange(nj * nb_max, dtype=I32)
        wc = jnp.minimum(w, nvalid - 1)
        e = jnp.clip(jnp.sum((start[None, :] <= wc[:, None]).astype(I32), axis=1) - 1, 0, N_EXPERTS - 1)
        local = wc - _table_lookup(start, e)
        nbe = jnp.maximum(_table_lookup(nb, e), 1)
        j = local // nbe
        il = local % nbe
        valid = w < nvalid
        first = (il == 0) & valid
        half = (il == nbe - 1) & (_table_lookup(odd, e) == 1)
        slot = (_table_lookup(groups_before, e) * nj + j) % 2
        same_e = j + 1 < nj
        e_next = jnp.where(same_e, e, _table_lookup(next_e, e))
        j_next = jnp.where(same_e, j + 1, 0)
        has_next = e_next < N_EXPERTS
        flags = (FLAG_VALID * valid + FLAG_FIRST * first + FLAG_HALF * half + FLAG_SLOT * slot
                 + FLAG_NEXT * has_next).astype(I32)
        lists.append((e, j.astype(I32), (_table_lookup(blk_off, e) + il).astype(I32), flags,
                      jnp.minimum(e_next, N_EXPERTS - 1).astype(I32), j_next.astype(I32)))
    tail_blk = jnp.minimum(2 * blk_off + jnp.maximum(n_half, 1) - 1, 2 * nb_max - 1).astype(I32)
    return lists, blk_off * MOE_BM, tail_blk


def _table_lookup(table, idx):
    onehot = idx[..., None] == jnp.arange(table.shape[0], dtype=I32)
    return jnp.sum(jnp.where(onehot, table, 0), axis=-1).astype(I32)


def _combine_kernel(pos_ref, y_ref, tp_ref, x1_ref, g2p_ref, g2s_ref, o_ref, buf, sem,
                    *, tb, n_prompt_blocks, blocks_per_batch):
    i = pl.program_id(0)
    slot = i % 2

    def row_copy(blk, sl, r, kk):
        p = pos_ref[(blk * tb + r) * TOP_K + kk]
        return pltpu.make_async_copy(y_ref.at[pl.ds(p, 1)], buf.at[sl, kk, pl.ds(r, 1)], sem.at[sl])

    def start_block(blk, sl):
        for r in range(tb):
            for kk in range(TOP_K):
                row_copy(blk, sl, r, kk).start()

    def wait_block(blk, sl):
        def body(r, c):
            for kk in range(TOP_K):
                row_copy(blk, sl, r, kk).wait()
            return c
        lax.fori_loop(0, tb, body, 0, unroll=4)

    @pl.when(i == 0)
    def _():
        start_block(0, 0)

    @pl.when(i + 1 < pl.num_programs(0))
    def _():
        start_block(i + 1, 1 - slot)

    wait_block(i, slot)
    tp = tp_ref[...]
    moe = buf[slot, 0] * tp[:, 0:1]
    for kk in range(1, TOP_K):
        moe = moe + buf[slot, kk] * tp[:, kk:kk + 1]
    brow = jnp.minimum(i // blocks_per_batch, 7)
    g2 = jnp.where(i >= n_prompt_blocks, g2s_ref[...], g2p_ref[pl.ds(brow, 1), :])
    o_ref[...] = x1_ref[...] + g2 * moe


def combine(pos_flat, y_sorted, topp, x1, mod, layer, *, n_prompt, rows_per_batch, prompt_mod_blk, tb=64):
    t, d = x1.shape
    npb = n_prompt // tb
    kern = functools.partial(_combine_kernel, tb=tb, n_prompt_blocks=npb,
                             blocks_per_batch=rows_per_batch // tb)
    return pl.pallas_call(
        kern,
        out_shape=jax.ShapeDtypeStruct((t, d), F32),
        grid_spec=pltpu.PrefetchScalarGridSpec(
            num_scalar_prefetch=1,
            grid=(t // tb,),
            in_specs=[
                pl.BlockSpec(memory_space=pl.ANY),
                pl.BlockSpec((tb, LANES), lambda i, pos: (i, 0)),
                pl.BlockSpec((tb, d), lambda i, pos: (i, 0)),
                pl.BlockSpec((None, 8, d), lambda i, pos: (layer, prompt_mod_blk, 5)),
                pl.BlockSpec((None, tb, d), lambda i, pos: (layer, jnp.maximum(i - npb, 0), 5)),
            ],
            out_specs=pl.BlockSpec((tb, d), lambda i, pos: (i, 0)),
            scratch_shapes=[pltpu.VMEM((2, TOP_K, tb, d), F32), pltpu.SemaphoreType.DMA((2,))],
        ),
        compiler_params=_cp(("arbitrary",)),
        name="moe_combine",
    )(pos_flat, y_sorted, topp, x1, mod, mod)


def _final_kernel(x_ref, w_ref, yp_ref, ys_ref, *, n_prompt_blocks):
    i = pl.program_id(0)
    x = x_ref[...]
    y = x * lax.rsqrt(jnp.mean(x * x, -1, keepdims=True) + EPS) * w_ref[...]

    @pl.when(i < n_prompt_blocks)
    def _():
        yp_ref[...] = y

    @pl.when(i >= n_prompt_blocks)
    def _():
        ys_ref[...] = y


def final_norm_split(x_all, w, *, n_prompt, tb):
    t, d = x_all.shape
    npb = n_prompt // tb
    return pl.pallas_call(
        functools.partial(_final_kernel, n_prompt_blocks=npb),
        out_shape=(jax.ShapeDtypeStruct((n_prompt, d), F32), jax.ShapeDtypeStruct((t - n_prompt, d), F32)),
        grid=(t // tb,),
        in_specs=[pl.BlockSpec((tb, d), lambda i: (i, 0)),
                  pl.BlockSpec((1, d), lambda i: (0, 0))],
        out_specs=(pl.BlockSpec((tb, d), lambda i: (jnp.minimum(i, npb - 1), 0)),
                   pl.BlockSpec((tb, d), lambda i: (jnp.maximum(i - npb, 0), 0))),
        compiler_params=_cp(("arbitrary",)),
        name="final_norm",
    )(x_all, w.reshape(1, d))


def kernel(x_prompt, x_sample, state_pool, state_shortconv, state_qkv_conv, state_delta,
           c_prompt, c_sample, norm1, norm2, w_ada, b_ada, w_in, pool_w, pool_scale,
           sc_conv_w, qkv_conv_w, a_log, dt_bias, onorm_w, w_o, w_router, b_router,
           w_gate_up, b_gate_up, w_down, b_down, final_norm):
    batch, seq, d = x_prompt.shape
    nb = x_sample.shape[0]
    depth = w_in.shape[0]
    n_prompt = batch * seq
    total = n_prompt + nb
    pool_width = pool_scale.shape[1]
    sc_width = sc_conv_w.shape[2]
    gdn_width = qkv_conv_w.shape[2] // 3
    col_sc = pool_width
    col_gdn = pool_width + 3 * sc_width
    tm_in = min(1024, seq)
    tm_o = min(512, seq)
    assert nb == LANES and seq % tm_in == 0 and seq % GDN_CHUNK == 0

    x_all = jnp.concatenate([x_prompt.reshape(n_prompt, d), x_sample.reshape(nb, d)], axis=0)
    c_all = jnp.concatenate([c_sample, c_prompt, jnp.zeros((8 - batch, d), F32)], axis=0)
    prompt_mod_blk = nb // 8
    mod = ada_mod(c_all, w_ada, b_ada)

    alog_row = _head_row(a_log, 0)
    dtb_row = _head_row(dt_bias, 0)
    w_o_bf = w_o.astype(BF16)
    wr_pad = jnp.zeros((depth, d, LANES), F32).at[:, :, :N_EXPERTS].set(w_router)
    br_pad = jnp.zeros((depth, 1, LANES), F32).at[:, 0, :N_EXPERTS].set(b_router)
    sp_t = jnp.swapaxes(state_pool, 1, 2)
    ssc_t = jnp.swapaxes(state_shortconv, 1, 2)
    sq_t = jnp.swapaxes(state_qkv_conv, 1, 2)

    nb_max = (total * TOP_K + N_EXPERTS * (MOE_BM - 1)) // MOE_BM
    npad = nb_max * MOE_BM
    dff = w_down.shape[2]
    tn_gu, tn_dn = 512, 1024
    w_in_bf = w_in.astype(BF16)

    pools_p, pools_s, scs_p, scs_s, qkvs_p, qkvs_s, deltas_p = ([] for _ in range(7))
    delta_s_all = None
    for l in range(depth):
        proj_p, ab_p = in_proj(x_all, mod, norm1, w_in_bf, l, row0=0, rows=n_prompt, tm=tm_in, tn=1024,
                               per_row=False, rows_per_batch=seq, mod_row_blk=prompt_mod_blk)
        proj_s, ab_s = in_proj(x_all, mod, norm1, w_in_bf, l, row0=n_prompt, rows=nb, tm=nb, tn=1024,
                               per_row=True, rows_per_batch=nb, mod_row_blk=0)
        ya_p = pool_prompt(proj_p, pool_w, pool_scale, l, batch=batch, seq=seq)
        yb_p, sc_new_p = sc_prompt(proj_p, sc_conv_w, l, batch=batch, seq=seq, width=sc_width, col0=col_sc)
        yc_p, s_new_p = gdn_prompt(proj_p, ab_p, qkv_conv_w, alog_row, dtb_row, onorm_w, l,
                                   batch=batch, seq=seq, col0=col_gdn)
        ya_s, yb_s, cx_s, qn, kn, vv, eg, beta = mix_sample(
            proj_s, ab_s, sp_t[l], ssc_t[l], sq_t[l], pool_w, pool_scale, sc_conv_w, qkv_conv_w,
            alog_row, dtb_row, l)
        yc_s, delta_s_all = delta_sample(qn, kn, vv, eg, beta, proj_s, onorm_w, state_delta, l,
                                         zcol_blk=(col_gdn + 3 * gdn_width) // gdn_width, prev=delta_s_all)

        p3 = proj_p.reshape(batch, seq, -1)
        pools_p.append(p3[:, seq - 15:, :pool_width])
        scs_p.append(sc_new_p)
        qkvs_p.append(p3[:, seq - 3:, col_gdn:col_gdn + 3 * gdn_width])
        deltas_p.append(s_new_p)
        pools_s.append(jnp.concatenate([state_pool[l][:, 1:], proj_s[:, None, :pool_width]], axis=1))
        scs_s.append(jnp.concatenate([state_shortconv[l][:, 1:], cx_s[:, None, :]], axis=1))
        qkvs_s.append(jnp.concatenate(
            [state_qkv_conv[l][:, 1:], proj_s[:, None, col_gdn:col_gdn + 3 * gdn_width]], axis=1))

        shared = o_proj(ya_p, yb_p, yc_p, x_all, mod, norm2, w_o_bf, wr_pad, br_pad, l,
                        row0=0, rows=n_prompt, tm=tm_o, per_row=False, rows_per_batch=seq,
                        mod_row_blk=prompt_mod_blk, total_rows=total)
        x1, xu, topi, topp = o_proj(ya_s, yb_s, yc_s, x_all, mod, norm2, w_o_bf, wr_pad, br_pad, l,
                                    row0=n_prompt, rows=nb, tm=nb, per_row=True, rows_per_batch=nb,
                                    mod_row_blk=0, total_rows=total, prev=shared)

        rank, cnt = route_rank(topi)
        counts = cnt[0, :N_EXPERTS]
        (work_gu, work_dn), row_off, tail_blk = make_work_lists(counts, (dff // tn_gu, d // tn_dn), nb_max)
        ti4 = topi[:, :TOP_K]
        pos_flat = (_table_lookup(row_off, ti4) + rank[:, :TOP_K]).reshape(-1).astype(I32)
        xs = dispatch(pos_flat, xu, zero_tail_blocks(tail_blk, npad, d // 2))
        act = moe_gate_up(work_gu, xs, w_gate_up, b_gate_up, l, tn=tn_gu)
        y_sorted = moe_down(work_dn, act, w_down, b_down, l, tn=tn_dn)
        x_all = combine(pos_flat, y_sorted, topp, x1, mod, l, n_prompt=n_prompt,
                        rows_per_batch=seq, prompt_mod_blk=prompt_mod_blk)

    y_p, y_s = final_norm_split(x_all, final_norm, n_prompt=n_prompt, tb=nb)
    return (y_p.reshape(batch, seq, d), y_s.reshape(nb, 1, d),
            jnp.stack(pools_p), jnp.stack(pools_s), jnp.stack(scs_p), jnp.stack(scs_s),
            jnp.stack(qkvs_p), jnp.stack(qkvs_s), jnp.stack(deltas_p), delta_s_all)
```

```python
import functools

import jax
import jax.numpy as jnp
from jax import lax
from jax.experimental import pallas as pl
from jax.experimental.pallas import tpu as pltpu

F32 = jnp.float32
BF16 = jnp.bfloat16
I32 = jnp.int32
U32 = jnp.uint32

EPS = 1e-6
POOL_WINDOWS = (2, 4, 8, 16)
POOL_GROUP = 128
GDN_HEADS = 8
GDN_DIM = 128
GDN_CHUNK = 64
GDN_STEP_CHUNKS = 4
N_EXPERTS = 32
TOP_K = 4
SWIGLU_LIMIT = 7.0
SWIGLU_ALPHA = 1.702
LANES = 128
MOE_BM = 512
MOE_HALF = MOE_BM // 2
VMEM_LIMIT = 56 * 1024 * 1024


def _cp(sem):
    return pltpu.CompilerParams(dimension_semantics=sem, vmem_limit_bytes=VMEM_LIMIT)


def _silu(x):
    return x * jax.nn.sigmoid(x)


def _softplus(x):
    return jnp.maximum(x, 0.0) + jnp.log1p(jnp.exp(-jnp.abs(x)))


def _bdot(a, b):
    return jnp.dot(a.astype(BF16), b.astype(BF16), preferred_element_type=F32)


def _shift_rows(x, s, row):
    return jnp.where(row >= s, pltpu.roll(x, s, 0), 0.0)


def _ada_kernel(c_ref, w_ref, b_ref, o_ref):
    o_ref[...] = _bdot(_silu(c_ref[...]), w_ref[...]) + b_ref[...]


def ada_mod(c_all, w_ada, b_ada, tn=1024):
    depth, d, n = w_ada.shape
    rows = c_all.shape[0]
    return pl.pallas_call(
        _ada_kernel,
        out_shape=jax.ShapeDtypeStruct((depth, rows, n), F32),
        grid=(depth, n // tn),
        in_specs=[
            pl.BlockSpec((rows, d), lambda l, j: (0, 0)),
            pl.BlockSpec((None, d, tn), lambda l, j: (l, 0, j)),
            pl.BlockSpec((None, 1, tn), lambda l, j: (l, 0, j)),
        ],
        out_specs=pl.BlockSpec((None, rows, tn), lambda l, j: (l, 0, j)),
        compiler_params=_cp(("arbitrary", "arbitrary")),
        name="ada_mod",
    )(c_all, w_ada, b_ada.reshape(depth, 1, n))


def _mod_rows(ref, per_row, brow):
    if per_row:
        return ref[...]
    return ref[pl.ds(brow, 1), :]


def _inproj_kernel(x_ref, sh_ref, sc_ref, n_ref, w_ref, wab_ref, o_ref, oab_ref, h_scr,
                   *, per_row, blocks_per_batch, n_ab):
    i = pl.program_id(0)
    j = pl.program_id(1)

    @pl.when(j == 0)
    def _():
        x = x_ref[...]
        y = x * lax.rsqrt(jnp.mean(x * x, -1, keepdims=True) + EPS) * n_ref[...]
        brow = i // blocks_per_batch
        h = y * (1.0 + _mod_rows(sc_ref, per_row, brow)) + _mod_rows(sh_ref, per_row, brow)
        hb = h.astype(BF16)
        h_scr[...] = hb
        col = lax.broadcasted_iota(I32, wab_ref.shape, 1)
        wab = jnp.where(col < n_ab, wab_ref[...], 0.0)
        oab_ref[...] = _bdot(hb, wab)

    o_ref[...] = jnp.dot(h_scr[...], w_ref[...], preferred_element_type=F32)


def in_proj(x_all, mod, norm_w, w_in, layer, *, row0, rows, tm, tn, per_row, rows_per_batch, mod_row_blk):
    depth, d, n_cols = w_in.shape
    n_main = (n_cols // LANES) * LANES
    n_ab = n_cols - n_main
    mrows = tm if per_row else 8
    mod_spec = lambda chunk: pl.BlockSpec((None, mrows, d), lambda i, j: (layer, mod_row_blk, chunk))
    kern = functools.partial(_inproj_kernel, per_row=per_row,
                             blocks_per_batch=max(rows_per_batch // tm, 1), n_ab=n_ab)
    return pl.pallas_call(
        kern,
        out_shape=(jax.ShapeDtypeStruct((rows, n_main), F32), jax.ShapeDtypeStruct((rows, LANES), F32)),
        grid=(rows // tm, n_main // tn),
        in_specs=[
            pl.BlockSpec((tm, d), lambda i, j: (row0 // tm + i, 0)),
            mod_spec(0), mod_spec(1),
            pl.BlockSpec((None, 1, d), lambda i, j: (layer, 0, 0)),
            pl.BlockSpec((None, d, tn), lambda i, j: (layer, 0, j)),
            pl.BlockSpec((None, d, LANES), lambda i, j: (layer, 0, n_main // LANES)),
        ],
        out_specs=(pl.BlockSpec((tm, tn), lambda i, j: (i, j)),
                   pl.BlockSpec((tm, LANES), lambda i, j: (i, 0))),
        scratch_shapes=[pltpu.VMEM((tm, d), BF16)],
        compiler_params=_cp(("arbitrary", "arbitrary")),
        name="in_proj",
    )(x_all, mod, mod, norm_w.reshape(depth, 1, d), w_in, w_in)


def _pool_p_kernel(u_ref, w_ref, s_ref, o_ref):
    g = pl.program_id(1)
    u = u_ref[...]
    row = lax.broadcasted_iota(I32, u.shape, 0)
    w2 = u + _shift_rows(u, 1, row)
    w4 = w2 + _shift_rows(w2, 2, row)
    w8 = w4 + _shift_rows(w4, 4, row)
    w16 = w8 + _shift_rows(w8, 8, row)
    wsum = jnp.where(g == 0, w2, jnp.where(g == 1, w4, jnp.where(g == 2, w8, w16)))
    win = jnp.left_shift(2, g)
    cnt = jnp.minimum(row + 1, win).astype(F32)
    d = wsum / cnt - u
    o_ref[...] = (_bdot(d, w_ref[...]) * s_ref[...]).astype(o_ref.dtype)


def pool_prompt(proj, pool_w, pool_scale, layer, *, batch, seq):
    depth = pool_w.shape[0]
    ng = len(POOL_WINDOWS)
    return pl.pallas_call(
        _pool_p_kernel,
        out_shape=jax.ShapeDtypeStruct((batch * seq, ng * POOL_GROUP), BF16),
        grid=(batch, ng),
        in_specs=[
            pl.BlockSpec((seq, POOL_GROUP), lambda b, g: (b, g)),
            pl.BlockSpec((None, None, POOL_GROUP, POOL_GROUP), lambda b, g: (layer, g, 0, 0)),
            pl.BlockSpec((None, 1, POOL_GROUP), lambda b, g: (layer, 0, g)),
        ],
        out_specs=pl.BlockSpec((seq, POOL_GROUP), lambda b, g: (b, g)),
        compiler_params=_cp(("arbitrary", "arbitrary")),
        name="pool_prompt",
    )(proj, pool_w, pool_scale.reshape(depth, 1, ng * POOL_GROUP))


def _sc_p_kernel(x_ref, b_ref, c_ref, w_ref, o_ref, ns_ref):
    cx = c_ref[...] * x_ref[...]
    row = lax.broadcasted_iota(I32, cx.shape, 0)
    w = w_ref[...]
    y = _shift_rows(cx, 2, row) * w[0:1] + _shift_rows(cx, 1, row) * w[1:2] + cx * w[2:3]
    o_ref[...] = (b_ref[...] * y).astype(o_ref.dtype)
    n = cx.shape[0]
    ns_ref[...] = cx[n - 2:n, :]


def sc_prompt(proj, sc_conv_w, layer, *, batch, seq, width, col0, tc=256):
    nct = width // tc
    cb = col0 // tc
    return pl.pallas_call(
        _sc_p_kernel,
        out_shape=(jax.ShapeDtypeStruct((batch * seq, width), BF16),
                   jax.ShapeDtypeStruct((batch, 2, width), F32)),
        grid=(batch, nct),
        in_specs=[
            pl.BlockSpec((seq, tc), lambda b, c: (b, cb + c)),
            pl.BlockSpec((seq, tc), lambda b, c: (b, cb + nct + c)),
            pl.BlockSpec((seq, tc), lambda b, c: (b, cb + 2 * nct + c)),
            pl.BlockSpec((None, 3, tc), lambda b, c: (layer, 0, c)),
        ],
        out_specs=(pl.BlockSpec((seq, tc), lambda b, c: (b, c)),
                   pl.BlockSpec((None, 2, tc), lambda b, c: (b, 0, c))),
        compiler_params=_cp(("arbitrary", "arbitrary")),
        name="sc_prompt",
    )(proj, proj, proj, sc_conv_w)


def _inv_unit_lower_minus_eye(lms):
    c = lms[0].shape[0]
    ps = [-lm for lm in lms]
    rs = list(ps)
    steps = max(c.bit_length() - 2, 0)
    for _ in range(steps):
        ps = [_bdot(p, p) for p in ps]
        rs = [r + p + _bdot(r, p) for r, p in zip(rs, ps)]
    return rs


def _gdn_p_kernel(qc_ref, kc_ref, vc_ref, qp_ref, kp_ref, vp_ref, z_ref, ab_ref,
                  cwq_ref, cwk_ref, cwv_ref, alog_ref, dtb_ref, on_ref,
                  o_ref, sout_ref, s_scr):
    n = pl.program_id(1)
    rows = qc_ref.shape[0]
    c = GDN_CHUNK
    hd = GDN_DIM

    @pl.when(n == 0)
    def _():
        s_scr[...] = jnp.zeros(s_scr.shape, F32)

    has_prev = n > 0

    def conv_silu(cur_ref, prev_ref, w_ref):
        prev = jnp.where(has_prev, prev_ref[...], 0.0)
        xx = jnp.concatenate([prev, cur_ref[...]], axis=0)
        w = w_ref[...]
        y = (xx[5:5 + rows] * w[0:1] + xx[6:6 + rows] * w[1:2]
             + xx[7:7 + rows] * w[2:3] + xx[8:8 + rows] * w[3:4])
        return _silu(y)

    q = conv_silu(qc_ref, qp_ref, cwq_ref)
    k = conv_silu(kc_ref, kp_ref, cwk_ref)
    v = conv_silu(vc_ref, vp_ref, cwv_ref)
    z = z_ref[...]

    ab = ab_ref[...]
    g_all = -jnp.exp(alog_ref[...]) * _softplus(ab + dtb_ref[...])
    beta_all = jax.nn.sigmoid(ab)
    row_in_chunk = lax.broadcasted_iota(I32, g_all.shape, 0) % c
    gc = g_all
    s = 1
    while s < c:
        gc = gc + _shift_rows(gc, s, row_in_chunk)
        s *= 2
    eg_all = jnp.exp(gc)

    ri = lax.broadcasted_iota(I32, (c, c), 0)
    ci = lax.broadcasted_iota(I32, (c, c), 1)
    incl = ri >= ci
    strict = ri > ci
    on = on_ref[...]

    heads = range(GDN_HEADS)
    chunks = range(rows // c)
    nt = (((1,), (1,)), ((), ()))
    tn_dims = (((0,), (0,)), ((), ()))
    qns, kns, rhss, lms, aintras = [], [], [], [], []
    for ch in chunks:
        rs_ = slice(ch * c, (ch + 1) * c)
        gct = gc[rs_].T
        for h in heads:
            sl = slice(h * hd, (h + 1) * hd)
            qh, kh, vh = q[rs_, sl], k[rs_, sl], v[rs_, sl]
            qn = qh * lax.rsqrt(jnp.sum(qh * qh, -1, keepdims=True) + EPS) * (hd ** -0.5)
            kn = kh * lax.rsqrt(jnp.sum(kh * kh, -1, keepdims=True) + EPS)
            gcol = gc[rs_, h:h + 1]
            grow = gct[h:h + 1, :]
            bcol = beta_all[rs_, GDN_HEADS + h:GDN_HEADS + h + 1]
            decay = jnp.where(incl, jnp.exp(jnp.minimum(gcol - grow, 0.0)), 0.0)
            kb = kn * bcol
            knb = kn.astype(BF16)
            kk = lax.dot_general(kb.astype(BF16), knb, nt, preferred_element_type=F32)
            qk = lax.dot_general(qn.astype(BF16), knb, nt, preferred_element_type=F32)
            lms.append(jnp.where(strict, kk * decay, 0.0))
            aintras.append(jnp.where(incl, qk * decay, 0.0))
            rhss.append(jnp.concatenate([vh * bcol, kb * eg_all[rs_, h:h + 1]], axis=-1))
            qns.append(qn)
            kns.append(kn)
    rs = _inv_unit_lower_minus_eye(lms)
    sols = [rhs + _bdot(r, rhs) for r, rhs in zip(rs, rhss)]
    for ch in chunks:
        rs_ = slice(ch * c, (ch + 1) * c)
        for h in heads:
            i = ch * GDN_HEADS + h
            sl = slice(h * hd, (h + 1) * hd)
            gcol = gc[rs_, h:h + 1]
            u_val, k_cum = sols[i][:, :hd], sols[i][:, hd:]
            s_old = s_scr[h]
            v_new = u_val - _bdot(k_cum, s_old)
            o = _bdot(qns[i] * eg_all[rs_, h:h + 1], s_old) + _bdot(aintras[i], v_new)
            glast = gc[(ch + 1) * c - 1:(ch + 1) * c, h:h + 1]
            kd = kns[i] * jnp.exp(glast - gcol)
            s_scr[h] = s_old * jnp.exp(glast) + lax.dot_general(
                kd.astype(BF16), v_new.astype(BF16), tn_dims, preferred_element_type=F32)
            o = o * lax.rsqrt(jnp.mean(o * o, -1, keepdims=True) + EPS) * on
            o_ref[rs_, sl] = (o * _silu(z[rs_, sl])).astype(o_ref.dtype)

    @pl.when(n == pl.num_programs(1) - 1)
    def _():
        sout_ref[...] = s_scr[...]


def _head_row(v, offset):
    depth, h = v.shape
    return jnp.zeros((depth, 1, LANES), F32).at[:, 0, offset:offset + h].set(v.astype(F32))


def gdn_prompt(proj, ab, qkv_conv_w, alog_row, dtb_row, onorm_w, layer, *, batch, seq, col0):
    depth = qkv_conv_w.shape[0]
    wdt = GDN_HEADS * GDN_DIM
    c = GDN_CHUNK * GDN_STEP_CHUNKS
    nchunk = seq // c
    cb = col0 // wdt
    cur = lambda t: pl.BlockSpec((c, wdt), lambda b, n: (b * nchunk + n, cb + t))
    prev = lambda t: pl.BlockSpec(
        (8, wdt), lambda b, n: (jnp.maximum((b * nchunk + n) * (c // 8) - 1, 0), cb + t))
    cw = lambda t: pl.BlockSpec((None, 4, wdt), lambda b, n: (layer, 0, t))
    row = pl.BlockSpec((None, 1, LANES), lambda b, n: (layer, 0, 0))
    return pl.pallas_call(
        _gdn_p_kernel,
        out_shape=(jax.ShapeDtypeStruct((batch * seq, wdt), BF16),
                   jax.ShapeDtypeStruct((batch, GDN_HEADS, GDN_DIM, GDN_DIM), F32)),
        grid=(batch, nchunk),
        in_specs=[cur(0), cur(1), cur(2), prev(0), prev(1), prev(2), cur(3),
                  pl.BlockSpec((c, LANES), lambda b, n: (b * nchunk + n, 0)),
                  cw(0), cw(1), cw(2), row, row, row],
        out_specs=(pl.BlockSpec((c, wdt), lambda b, n: (b * nchunk + n, 0)),
                   pl.BlockSpec((None, GDN_HEADS, GDN_DIM, GDN_DIM), lambda b, n: (b, 0, 0, 0))),
        scratch_shapes=[pltpu.VMEM((GDN_HEADS, GDN_DIM, GDN_DIM), F32)],
        compiler_params=_cp(("arbitrary", "arbitrary")),
        name="gdn_prompt",
    )(proj, proj, proj, proj, proj, proj, proj, ab,
      qkv_conv_w, qkv_conv_w, qkv_conv_w, alog_row, dtb_row,
      onorm_w.reshape(depth, 1, GDN_DIM))


def _mix_s_kernel(proj_ref, ab_ref, sp_ref, ssc_ref, sq_ref, pw_ref, ps_ref, scw_ref, cw_ref,
                  alog_ref, dtb_ref,
                  ya_ref, yb_ref, cx_ref, q_ref, k_ref, v_ref, eg_ref, beta_ref,
                  *, pool_w, sc_w, gdn_w):
    for g, win in enumerate(POOL_WINDOWS):
        sl = slice(g * POOL_GROUP, (g + 1) * POOL_GROUP)
        u = proj_ref[:, sl]
        acc = u
        nbuf = sp_ref.shape[0]
        for r in range(nbuf - (win - 1), nbuf):
            acc = acc + sp_ref[r, :, sl]
        d = acc / float(win) - u
        ya_ref[:, sl] = (_bdot(d, pw_ref[g]) * ps_ref[:, sl]).astype(ya_ref.dtype)
    c0 = pool_w
    xs = proj_ref[:, c0:c0 + sc_w]
    bg = proj_ref[:, c0 + sc_w:c0 + 2 * sc_w]
    cg = proj_ref[:, c0 + 2 * sc_w:c0 + 3 * sc_w]
    cx = cg * xs
    w = scw_ref[...]
    y = ssc_ref[0] * w[0:1] + ssc_ref[1] * w[1:2] + cx * w[2:3]
    yb_ref[...] = (bg * y).astype(yb_ref.dtype)
    cx_ref[...] = cx
    c1 = c0 + 3 * sc_w
    hd = GDN_DIM
    outs = (q_ref, k_ref, v_ref)
    for t in range(3):
        new = proj_ref[:, c1 + t * gdn_w:c1 + (t + 1) * gdn_w]
        ws = slice(t * gdn_w, (t + 1) * gdn_w)
        y = (sq_ref[0, :, ws] * cw_ref[0:1, ws] + sq_ref[1, :, ws] * cw_ref[1:2, ws]
             + sq_ref[2, :, ws] * cw_ref[2:3, ws] + new * cw_ref[3:4, ws])
        y = _silu(y)
        for h in range(GDN_HEADS):
            sl = slice(h * hd, (h + 1) * hd)
            yh = y[:, sl]
            if t == 0:
                yh = yh * lax.rsqrt(jnp.sum(yh * yh, -1, keepdims=True) + EPS) * (hd ** -0.5)
            elif t == 1:
                yh = yh * lax.rsqrt(jnp.sum(yh * yh, -1, keepdims=True) + EPS)
            outs[t][:, sl] = yh
    ab = ab_ref[...]
    eg_ref[...] = jnp.exp(-jnp.exp(alog_ref[...]) * _softplus(ab + dtb_ref[...]))
    beta_ref[...] = jax.nn.sigmoid(ab)


def mix_sample(proj, ab, sp_t, ssc_t, sq_t, pool_w, pool_scale, sc_conv_w, qkv_conv_w,
               alog_row, dtb_row, layer):
    nb = proj.shape[0]
    depth = pool_w.shape[0]
    pw = pool_scale.shape[1]
    scw = sc_conv_w.shape[2]
    gw = qkv_conv_w.shape[2] // 3
    full = lambda a: pl.BlockSpec(a.shape, lambda i: (0,) * a.ndim)
    lay = lambda a: pl.BlockSpec((None,) + a.shape[1:], lambda i: (layer,) + (0,) * (a.ndim - 1))
    ps3 = pool_scale.reshape(depth, 1, pw)
    kern = functools.partial(_mix_s_kernel, pool_w=pw, sc_w=scw, gdn_w=gw)
    shp = lambda w, dt: jax.ShapeDtypeStruct((nb, w), dt)
    outs = (shp(pw, BF16), shp(scw, BF16), shp(scw, F32), shp(gw, F32), shp(gw, F32), shp(gw, F32),
            shp(LANES, F32), shp(LANES, F32))
    return pl.pallas_call(
        kern,
        out_shape=outs,
        grid=(1,),
        in_specs=[full(proj), full(ab), full(sp_t), full(ssc_t), full(sq_t),
                  lay(pool_w), lay(ps3), lay(sc_conv_w), lay(qkv_conv_w), lay(alog_row), lay(dtb_row)],
        out_specs=tuple(pl.BlockSpec(o.shape, lambda i: (0, 0)) for o in outs),
        compiler_params=_cp(("arbitrary",)),
        name="mix_sample",
    )(proj, ab, sp_t, ssc_t, sq_t, pool_w, ps3, sc_conv_w, qkv_conv_w, alog_row, dtb_row)


def _delta_s_kernel(q_ref, k_ref, v_ref, eg_ref, beta_ref, z_ref, on_ref, s_ref, *rest, bb):
    o_ref, so_ref = rest[-2:]
    i = pl.program_id(0)
    hd = GDN_DIM
    on = on_ref[...]
    tn_dims = (((0,), (0,)), ((), ()))
    heads = range(GDN_HEADS)

    def body(r, carry):
        t = i * bb + r
        qrow = q_ref[pl.ds(t, 1), :]
        krow = k_ref[pl.ds(t, 1), :]
        vrow = v_ref[pl.ds(t, 1), :]
        egrow = eg_ref[pl.ds(t, 1), :]
        brow = beta_ref[pl.ds(t, 1), :]
        zrow = z_ref[pl.ds(t, 1), :]
        ress = []
        for h in heads:
            sl = slice(h * hd, (h + 1) * hd)
            kq = jnp.concatenate([krow[:, sl], qrow[:, sl], jnp.zeros((6, hd), F32)], axis=0)
            ress.append(_bdot(kq, s_ref[r, h]))
        upds = []
        for h in heads:
            sl = slice(h * hd, (h + 1) * hd)
            qh, kh, vh = qrow[:, sl], krow[:, sl], vrow[:, sl]
            eg = egrow[:, h:h + 1]
            bt = brow[:, GDN_HEADS + h:GDN_HEADS + h + 1]
            delta = (vh - eg * ress[h][0:1]) * bt
            qk = jnp.sum(qh * kh, -1, keepdims=True)
            o = eg * ress[h][1:2] + qk * delta
            o = o * lax.rsqrt(jnp.mean(o * o, -1, keepdims=True) + EPS) * on
            o_ref[r, :, sl] = o * _silu(zrow[:, sl])
            khi = kh.astype(BF16).astype(F32)
            dhi = delta.astype(BF16).astype(F32)
            zpad = jnp.zeros((13, hd), F32)
            lhs = jnp.concatenate([khi, kh - khi, khi, zpad], axis=0).astype(BF16)
            rhs = jnp.concatenate([dhi, dhi, delta - dhi, zpad], axis=0).astype(BF16)
            upds.append(lax.dot_general(lhs, rhs, tn_dims, preferred_element_type=F32))
        for h in heads:
            so_ref[r, h] = s_ref[r, h] * egrow[:, h:h + 1] + upds[h]
        return carry

    lax.fori_loop(0, bb, body, 0)


def delta_sample(qn, kn, vv, eg, beta, proj, onorm_w, state_delta, layer, *, zcol_blk, prev=None, bb=8):
    nb, gw = qn.shape
    depth = onorm_w.shape[0]
    full = lambda a: pl.BlockSpec(a.shape, lambda i: (0, 0))
    sblk = (None, bb, GDN_HEADS, GDN_DIM, GDN_DIM)
    in_specs = [full(qn), full(kn), full(vv), full(eg), full(beta),
                pl.BlockSpec((nb, gw), lambda i: (0, zcol_blk)),
                pl.BlockSpec((None, 1, GDN_DIM), lambda i: (layer, 0, 0)),
                pl.BlockSpec(sblk, lambda i: (layer, i, 0, 0, 0))]
    args = [qn, kn, vv, eg, beta, proj, onorm_w.reshape(depth, 1, GDN_DIM), state_delta]
    aliases = {}
    if prev is not None:
        in_specs.append(pl.BlockSpec(memory_space=pl.ANY))
        args.append(prev)
        aliases = {len(args) - 1: 1}
    o, s_new = pl.pallas_call(
        functools.partial(_delta_s_kernel, bb=bb),
        out_shape=(jax.ShapeDtypeStruct((nb, 1, gw), F32),
                   jax.ShapeDtypeStruct(state_delta.shape, F32)),
        grid=(nb // bb,),
        in_specs=in_specs,
        out_specs=(pl.BlockSpec((bb, 1, gw), lambda i: (i, 0, 0)),
                   pl.BlockSpec(sblk, lambda i: (layer, i, 0, 0, 0))),
        input_output_aliases=aliases,
        compiler_params=_cp(("arbitrary",)),
        name="delta_sample",
    )(*args)
    return o.reshape(nb, gw), s_new


def _pack_halves(h):
    half = h.shape[1] // 2
    lo = pltpu.bitcast(h[:, :half].astype(BF16).astype(F32), U32)
    hi = pltpu.bitcast(h[:, half:].astype(BF16).astype(F32), U32)
    return (lo >> 16) | (hi & jnp.uint32(0xFFFF0000))


def _unpack_halves(xu):
    lo = pltpu.bitcast(xu << 16, F32).astype(BF16)
    hi = pltpu.bitcast(xu & jnp.uint32(0xFFFF0000), F32).astype(BF16)
    return lo, hi


def _oproj_kernel(*refs, per_row, blocks_per_batch, n_alias):
    (ya_ref, yb_ref, yc_ref, x_ref, g1_ref, sh_ref, sc_ref, n_ref,
     woa_ref, wob_ref, woc_ref, wr_ref, br_ref) = refs[:13]
    x1_ref, xu_ref, ti_ref, tp_ref = refs[13 + n_alias:13 + n_alias + 4]
    i = pl.program_id(0)
    brow = i // blocks_per_batch
    acc = (jnp.dot(ya_ref[...].astype(BF16), woa_ref[...], preferred_element_type=F32)
           + jnp.dot(yb_ref[...].astype(BF16), wob_ref[...], preferred_element_type=F32)
           + jnp.dot(yc_ref[...].astype(BF16), woc_ref[...], preferred_element_type=F32))
    x1 = x_ref[...] + _mod_rows(g1_ref, per_row, brow) * acc
    x1_ref[...] = x1
    y = x1 * lax.rsqrt(jnp.mean(x1 * x1, -1, keepdims=True) + EPS) * n_ref[...]
    h = y * (1.0 + _mod_rows(sc_ref, per_row, brow)) + _mod_rows(sh_ref, per_row, brow)
    xu_ref[...] = _pack_halves(h)
    hhi = h.astype(BF16)
    hlo = (h - hhi.astype(F32)).astype(BF16)
    wr = wr_ref[...]
    whi = wr.astype(BF16)
    wlo = (wr - whi.astype(F32)).astype(BF16)
    logits = (jnp.dot(hhi, whi, preferred_element_type=F32)
              + jnp.dot(hlo, whi, preferred_element_type=F32)
              + jnp.dot(hhi, wlo, preferred_element_type=F32)) + br_ref[...]
    lane = lax.broadcasted_iota(I32, logits.shape, 1)
    neg = jnp.float32(-jnp.inf)
    cur = jnp.where(lane < N_EXPERTS, logits, neg)
    ti = jnp.zeros(logits.shape, I32)
    tv = jnp.full(logits.shape, neg, F32)
    for kk in range(TOP_K):
        m = jnp.max(cur, -1, keepdims=True)
        idx = jnp.min(jnp.where(cur == m, lane, LANES), -1, keepdims=True)
        ti = jnp.where(lane == kk, idx, ti)
        tv = jnp.where(lane == kk, m, tv)
        cur = jnp.where(lane == idx, neg, cur)
    e = jnp.exp(tv - jnp.max(tv, -1, keepdims=True))
    ti_ref[...] = ti
    tp_ref[...] = e / jnp.sum(e, -1, keepdims=True)


def o_proj(ya, yb, yc, x_all, mod, norm_w, w_o_bf, w_router_pad, b_router_pad, layer, *,
           row0, rows, tm, per_row, rows_per_batch, mod_row_blk, total_rows, prev=None):
    depth, d, _ = w_o_bf.shape
    wa, wb, wc = ya.shape[1], yb.shape[1], yc.shape[1]
    mrows = tm if per_row else 8
    rb = row0 // tm
    mod_spec = lambda chunk: pl.BlockSpec((None, mrows, d), lambda i: (layer, mod_row_blk, chunk))
    n_alias = 0 if prev is None else 4
    kern = functools.partial(_oproj_kernel, per_row=per_row,
                             blocks_per_batch=max(rows_per_batch // tm, 1), n_alias=n_alias)
    in_specs = [
        pl.BlockSpec((tm, wa), lambda i: (i, 0)),
        pl.BlockSpec((tm, wb), lambda i: (i, 0)),
        pl.BlockSpec((tm, wc), lambda i: (i, 0)),
        pl.BlockSpec((tm, d), lambda i: (rb + i, 0)),
        mod_spec(2), mod_spec(3), mod_spec(4),
        pl.BlockSpec((None, 1, d), lambda i: (layer, 0, 0)),
        pl.BlockSpec((None, wa, d), lambda i: (layer, 0, 0)),
        pl.BlockSpec((None, wb, d), lambda i: (layer, wa // wb, 0)),
        pl.BlockSpec((None, wc, d), lambda i: (layer, (wa + wb) // wc, 0)),
        pl.BlockSpec((None, d, LANES), lambda i: (layer, 0, 0)),
        pl.BlockSpec((None, 1, LANES), lambda i: (layer, 0, 0)),
    ]
    args = [ya, yb, yc, x_all, mod, mod, mod, norm_w.reshape(depth, 1, d),
            w_o_bf, w_o_bf, w_o_bf, w_router_pad, b_router_pad]
    aliases = {}
    if prev is not None:
        in_specs += [pl.BlockSpec(memory_space=pl.ANY)] * 4
        args += list(prev)
        aliases = {13 + t: t for t in range(4)}
    out_shape = (jax.ShapeDtypeStruct((total_rows, d), F32),
                 jax.ShapeDtypeStruct((total_rows, d // 2), U32),
                 jax.ShapeDtypeStruct((total_rows, LANES), I32),
                 jax.ShapeDtypeStruct((total_rows, LANES), F32))
    out_specs = (pl.BlockSpec((tm, d), lambda i: (rb + i, 0)),
                 pl.BlockSpec((tm, d // 2), lambda i: (rb + i, 0)),
                 pl.BlockSpec((tm, LANES), lambda i: (rb + i, 0)),
                 pl.BlockSpec((tm, LANES), lambda i: (rb + i, 0)))
    return pl.pallas_call(
        kern, out_shape=out_shape, grid=(rows // tm,),
        in_specs=in_specs, out_specs=out_specs,
        input_output_aliases=aliases,
        compiler_params=_cp(("arbitrary",)),
        name="o_proj",
    )(*args)


def _rank_kernel(ti_ref, rank_ref, cnt_ref, run_scr):
    i = pl.program_id(0)

    @pl.when(i == 0)
    def _():
        run_scr[...] = jnp.zeros(run_scr.shape, F32)

    ti = ti_ref[...]
    tb = ti.shape[0]
    lane = lax.broadcasted_iota(I32, ti.shape, 1)
    sel = jnp.zeros(ti.shape, F32)
    for kk in range(TOP_K):
        sel = sel + jnp.where(lane == ti[:, kk:kk + 1], 1.0, 0.0)
    r = lax.broadcasted_iota(I32, (tb, tb), 0)
    q = lax.broadcasted_iota(I32, (tb, tb), 1)
    tri = jnp.where(r > q, 1.0, 0.0)
    before = _bdot(tri, sel) + run_scr[0:1, :]
    rank = jnp.zeros(ti.shape, F32)
    for kk in range(TOP_K):
        rk = jnp.sum(jnp.where(lane == ti[:, kk:kk + 1], before, 0.0), -1, keepdims=True)
        rank = jnp.where(lane == kk, rk, rank)
    rank_ref[...] = rank.astype(I32)
    total = run_scr[0:1, :] + jnp.sum(sel, 0, keepdims=True)
    run_scr[...] = jnp.broadcast_to(total, run_scr.shape)
    cnt_ref[...] = jnp.broadcast_to(total, cnt_ref.shape).astype(I32)


def route_rank(topi, tb=128):
    t = topi.shape[0]
    return pl.pallas_call(
        _rank_kernel,
        out_shape=(jax.ShapeDtypeStruct((t, LANES), I32), jax.ShapeDtypeStruct((8, LANES), I32)),
        grid=(t // tb,),
        in_specs=[pl.BlockSpec((tb, LANES), lambda i: (i, 0))],
        out_specs=(pl.BlockSpec((tb, LANES), lambda i: (i, 0)),
                   pl.BlockSpec((8, LANES), lambda i: (0, 0))),
        scratch_shapes=[pltpu.VMEM((8, LANES), F32)],
        compiler_params=_cp(("arbitrary",)),
        name="route_rank",
    )(topi)


def _dispatch_kernel(pos_ref, xu_ref, xs_in_ref, xs_ref, sem, *, tb):
    del xs_in_ref
    i = pl.program_id(0)
    base = i * tb * TOP_K

    def row_copy(r, kk):
        p = pos_ref[base + r * TOP_K + kk]
        return pltpu.make_async_copy(xu_ref.at[pl.ds(r, 1)], xs_ref.at[pl.ds(p, 1)], sem)

    def wait(r, c):
        for kk in range(TOP_K):
            row_copy(r, kk).wait()
        return c

    for r in range(tb):
        for kk in range(TOP_K):
            row_copy(r, kk).start()
    lax.fori_loop(0, tb, wait, 0, unroll=4)


def _zero_tail_kernel(blk_ref, o_ref):
    del blk_ref
    o_ref[...] = jnp.zeros(o_ref.shape, o_ref.dtype)


def zero_tail_blocks(tail_blk, npad, width):
    return pl.pallas_call(
        _zero_tail_kernel,
        out_shape=jax.ShapeDtypeStruct((npad, width), U32),
        grid_spec=pltpu.PrefetchScalarGridSpec(
            num_scalar_prefetch=1,
            grid=(tail_blk.shape[0],),
            in_specs=[],
            out_specs=pl.BlockSpec((MOE_HALF, width), lambda e, blk: (blk[e], 0)),
        ),
        compiler_params=_cp(("arbitrary",)),
        name="moe_zero_tail",
    )(tail_blk)


def dispatch(pos_flat, xu, xs_zero, tb=128):
    t, w = xu.shape
    return pl.pallas_call(
        functools.partial(_dispatch_kernel, tb=tb),
        out_shape=jax.ShapeDtypeStruct(xs_zero.shape, xs_zero.dtype),
        grid_spec=pltpu.PrefetchScalarGridSpec(
            num_scalar_prefetch=1,
            grid=(t // tb,),
            in_specs=[pl.BlockSpec((tb, w), lambda i, pos: (i, 0)),
                      pl.BlockSpec(memory_space=pl.ANY)],
            out_specs=pl.BlockSpec(memory_space=pl.ANY),
            scratch_shapes=[pltpu.SemaphoreType.DMA(())],
        ),
        input_output_aliases={2: 0},
        compiler_params=_cp(("arbitrary",)),
        name="moe_dispatch",
    )(pos_flat, xu, xs_zero)


FLAG_VALID, FLAG_FIRST, FLAG_HALF, FLAG_SLOT, FLAG_NEXT = 1, 2, 4, 8, 16


def _expert_mm_kernel(we_ref, wj_ref, wi_ref, wfl_ref, wne_ref, wnj_ref, *refs, layer, tn, up_col0):
    gate_up = up_col0 is not None
    if gate_up:
        x_ref, w_hbm, bg_ref, bu_ref, o_ref, wf32, wbf, sem = refs
    else:
        x_ref, w_hbm, bd_ref, o_ref, wf32, wbf, sem = refs
    ntile = 2 if gate_up else 1
    w = pl.program_id(0)
    fl = wfl_ref[w]
    has = lambda bit: (fl & bit) != 0
    slot = jnp.where(has(FLAG_SLOT), 1, 0)

    def wcopy(e, j, sl, t):
        col = pl.multiple_of(j * tn + (up_col0 if t else 0), tn)
        return pltpu.make_async_copy(w_hbm.at[layer, e, :, pl.ds(col, tn)], wf32.at[sl, t], sem.at[sl, t])

    @pl.when(w == 0)
    def _():
        for t in range(ntile):
            wcopy(we_ref[0], wj_ref[0], 0, t).start()

    @pl.when(has(FLAG_FIRST))
    def _():
        for t in range(ntile):
            wcopy(we_ref[w], wj_ref[w], slot, t).wait()

        @pl.when(has(FLAG_NEXT))
        def _():
            for t in range(ntile):
                wcopy(wne_ref[w], wnj_ref[w], 1 - slot, t).start()

        for t in range(ntile):
            wbf[t] = wf32[slot, t].astype(BF16)

    def compute(m):
        if gate_up:
            lo, hi = _unpack_halves(x_ref[0:m, :])
            half = lo.shape[1]
            gate = (jnp.dot(lo, wbf[0, :half], preferred_element_type=F32)
                    + jnp.dot(hi, wbf[0, half:], preferred_element_type=F32) + bg_ref[...])
            up = (jnp.dot(lo, wbf[1, :half], preferred_element_type=F32)
                  + jnp.dot(hi, wbf[1, half:], preferred_element_type=F32) + bu_ref[...])
            gate = jnp.minimum(gate, SWIGLU_LIMIT)
            up = jnp.clip(up, -SWIGLU_LIMIT, SWIGLU_LIMIT)
            o_ref[0:m, :] = ((up + 1.0) * gate * jax.nn.sigmoid(SWIGLU_ALPHA * gate)).astype(o_ref.dtype)
        else:
            o_ref[0:m, :] = jnp.dot(x_ref[0:m, :], wbf[0], preferred_element_type=F32) + bd_ref[...]

    @pl.when(has(FLAG_VALID) & jnp.logical_not(has(FLAG_HALF)))
    def _():
        compute(MOE_BM)

    @pl.when(has(FLAG_VALID) & has(FLAG_HALF))
    def _():
        compute(MOE_HALF)


def _expert_mm(work, x, w, bias, layer, *, tn, gate_up, out_dtype):
    depth, ne, k, n_all = w.shape
    n_out = n_all // 2 if gate_up else n_all
    nj = n_out // tn
    npad = x.shape[0]
    nwork = work[0].shape[0]
    b4 = bias.reshape(depth, ne, 1, n_all)
    ntile = 2 if gate_up else 1
    x_spec = pl.BlockSpec((MOE_BM, x.shape[1]), lambda w_, we, wj, wi, *_: (wi[w_], 0))
    b_spec = lambda off: pl.BlockSpec((None, None, 1, tn), lambda w_, we, wj, *_: (layer, we[w_], 0, off + wj[w_]))
    in_specs = [x_spec, pl.BlockSpec(memory_space=pl.ANY), b_spec(0)] + ([b_spec(nj)] if gate_up else [])
    args = [x, w, b4] + ([b4] if gate_up else [])
    kern = functools.partial(_expert_mm_kernel, layer=layer, tn=tn, up_col0=n_out if gate_up else None)
    return pl.pallas_call(
        kern,
        out_shape=jax.ShapeDtypeStruct((npad, n_out), out_dtype),
        grid_spec=pltpu.PrefetchScalarGridSpec(
            num_scalar_prefetch=6,
            grid=(nwork,),
            in_specs=in_specs,
            out_specs=pl.BlockSpec((MOE_BM, tn), lambda w_, we, wj, wi, *_: (wi[w_], wj[w_])),
            scratch_shapes=[pltpu.VMEM((2, ntile, k, tn), F32), pltpu.VMEM((ntile, k, tn), BF16),
                            pltpu.SemaphoreType.DMA((2, ntile))],
        ),
        compiler_params=_cp(("arbitrary",)),
        name="moe_gate_up" if gate_up else "moe_down",
    )(*work, *args)


def moe_gate_up(work, xs, w_gate_up, b_gate_up, layer, *, tn):
    return _expert_mm(work, xs, w_gate_up, b_gate_up, layer, tn=tn, gate_up=True, out_dtype=BF16)


def moe_down(work, act, w_down, b_down, layer, *, tn):
    return _expert_mm(work, act, w_down, b_down, layer, tn=tn, gate_up=False, out_dtype=F32)


def make_work_lists(counts, njs, nb_max):
    n_half = (counts + MOE_HALF - 1) // MOE_HALF
    nb = (n_half + 1) // 2
    odd = n_half % 2
    blk_off = jnp.cumsum(nb) - nb
    total = jnp.sum(nb)
    nonempty = (nb > 0).astype(I32)
    groups_before = jnp.cumsum(nonempty) - nonempty
    ar = jnp.arange(N_EXPERTS, dtype=I32)
    later = (ar[None, :] > ar[:, None]) & (nb[None, :] > 0)
    next_e = jnp.min(jnp.where(later, ar[None, :], N_EXPERTS), axis=1)
    lists = []
    for nj in njs:
        start = nj * blk_off
        nvalid = nj * total
        w = jnp.arange(nj * nb_max, dtype=I32)
        wc = jnp.minimum(w, nvalid - 1)
        e = jnp.clip(jnp.sum((start[None, :] <= wc[:, None]).astype(I32), axis=1) - 1, 0, N_EXPERTS - 1)
        local = wc - _table_lookup(start, e)
        nbe = jnp.maximum(_table_lookup(nb, e), 1)
        j = local // nbe
        il = local % nbe
        valid = w < nvalid
        first = (il == 0) & valid
        half = (il == nbe - 1) & (_table_lookup(odd, e) == 1)
        slot = (_table_lookup(groups_before, e) * nj + j) % 2
        same_e = j + 1 < nj
        e_next = jnp.where(same_e, e, _table_lookup(next_e, e))
        j_next = jnp.where(same_e, j + 1, 0)
        has_next = e_next < N_EXPERTS
        flags = (FLAG_VALID * valid + FLAG_FIRST * first + FLAG_HALF * half + FLAG_SLOT * slot
                 + FLAG_NEXT * has_next).astype(I32)
        lists.append((e, j.astype(I32), (_table_lookup(blk_off, e) + il).astype(I32), flags,
                      jnp.minimum(e_next, N_EXPERTS - 1).astype(I32), j_next.astype(I32)))
    tail_blk = jnp.minimum(2 * blk_off + jnp.maximum(n_half, 1) - 1, 2 * nb_max - 1).astype(I32)
    return lists, blk_off * MOE_BM, tail_blk


def _table_lookup(table, idx):
    onehot = idx[..., None] == jnp.arange(table.shape[0], dtype=I32)
    return jnp.sum(jnp.where(onehot, table, 0), axis=-1).astype(I32)


def _combine_kernel(pos_ref, y_ref, tp_ref, x1_ref, g2p_ref, g2s_ref, o_ref, buf, sem,
                    *, tb, n_prompt_blocks, blocks_per_batch):
    i = pl.program_id(0)
    slot = i % 2

    def row_copy(blk, sl, r, kk):
        p = pos_ref[(blk * tb + r) * TOP_K + kk]
        return pltpu.make_async_copy(y_ref.at[pl.ds(p, 1)], buf.at[sl, kk, pl.ds(r, 1)], sem.at[sl])

    def start_block(blk, sl):
        for r in range(tb):
            for kk in range(TOP_K):
                row_copy(blk, sl, r, kk).start()

    def wait_block(blk, sl):
        def body(r, c):
            for kk in range(TOP_K):
                row_copy(blk, sl, r, kk).wait()
            return c
        lax.fori_loop(0, tb, body, 0, unroll=4)

    @pl.when(i == 0)
    def _():
        start_block(0, 0)

    @pl.when(i + 1 < pl.num_programs(0))
    def _():
        start_block(i + 1, 1 - slot)

    wait_block(i, slot)
    tp = tp_ref[...]
    moe = buf[slot, 0] * tp[:, 0:1]
    for kk in range(1, TOP_K):
        moe = moe + buf[slot, kk] * tp[:, kk:kk + 1]
    brow = jnp.minimum(i // blocks_per_batch, 7)
    g2 = jnp.where(i >= n_prompt_blocks, g2s_ref[...], g2p_ref[pl.ds(brow, 1), :])
    o_ref[...] = x1_ref[...] + g2 * moe


def combine(pos_flat, y_sorted, topp, x1, mod, layer, *, n_prompt, rows_per_batch, prompt_mod_blk, tb=64):
    t, d = x1.shape
    npb = n_prompt // tb
    kern = functools.partial(_combine_kernel, tb=tb, n_prompt_blocks=npb,
                             blocks_per_batch=rows_per_batch // tb)
    return pl.pallas_call(
        kern,
        out_shape=jax.ShapeDtypeStruct((t, d), F32),
        grid_spec=pltpu.PrefetchScalarGridSpec(
            num_scalar_prefetch=1,
            grid=(t // tb,),
            in_specs=[
                pl.BlockSpec(memory_space=pl.ANY),
                pl.BlockSpec((tb, LANES), lambda i, pos: (i, 0)),
                pl.BlockSpec((tb, d), lambda i, pos: (i, 0)),
                pl.BlockSpec((None, 8, d), lambda i, pos: (layer, prompt_mod_blk, 5)),
                pl.BlockSpec((None, tb, d), lambda i, pos: (layer, jnp.maximum(i - npb, 0), 5)),
            ],
            out_specs=pl.BlockSpec((tb, d), lambda i, pos: (i, 0)),
            scratch_shapes=[pltpu.VMEM((2, TOP_K, tb, d), F32), pltpu.SemaphoreType.DMA((2,))],
        ),
        compiler_params=_cp(("arbitrary",)),
        name="moe_combine",
    )(pos_flat, y_sorted, topp, x1, mod, mod)


def _final_kernel(x_ref, w_ref, yp_ref, ys_ref, *, n_prompt_blocks):
    i = pl.program_id(0)
    x = x_ref[...]
    y = x * lax.rsqrt(jnp.mean(x * x, -1, keepdims=True) + EPS) * w_ref[...]

    @pl.when(i < n_prompt_blocks)
    def _():
        yp_ref[...] = y

    @pl.when(i >= n_prompt_blocks)
    def _():
        ys_ref[...] = y


def final_norm_split(x_all, w, *, n_prompt, tb):
    t, d = x_all.shape
    npb = n_prompt // tb
    return pl.pallas_call(
        functools.partial(_final_kernel, n_prompt_blocks=npb),
        out_shape=(jax.ShapeDtypeStruct((n_prompt, d), F32), jax.ShapeDtypeStruct((t - n_prompt, d), F32)),
        grid=(t // tb,),
        in_specs=[pl.BlockSpec((tb, d), lambda i: (i, 0)),
                  pl.BlockSpec((1, d), lambda i: (0, 0))],
        out_specs=(pl.BlockSpec((tb, d), lambda i: (jnp.minimum(i, npb - 1), 0)),
                   pl.BlockSpec((tb, d), lambda i: (jnp.maximum(i - npb, 0), 0))),
        compiler_params=_cp(("arbitrary",)),
        name="final_norm",
    )(x_all, w.reshape(1, d))


def kernel(x_prompt, x_sample, state_pool, state_shortconv, state_qkv_conv, state_delta,
           c_prompt, c_sample, norm1, norm2, w_ada, b_ada, w_in, pool_w, pool_scale,
           sc_conv_w, qkv_conv_w, a_log, dt_bias, onorm_w, w_o, w_router, b_router,
           w_gate_up, b_gate_up, w_down, b_down, final_norm):
    batch, seq, d = x_prompt.shape
    nb = x_sample.shape[0]
    depth = w_in.shape[0]
    n_prompt = batch * seq
    total = n_prompt + nb
    pool_width = pool_scale.shape[1]
    sc_width = sc_conv_w.shape[2]
    gdn_width = qkv_conv_w.shape[2] // 3
    col_sc = pool_width
    col_gdn = pool_width + 3 * sc_width
    tm_in = min(1024, seq)
    tm_o = min(512, seq)
    assert nb == LANES and seq % tm_in == 0 and seq % GDN_CHUNK == 0

    x_all = jnp.concatenate([x_prompt.reshape(n_prompt, d), x_sample.reshape(nb, d)], axis=0)
    c_all = jnp.concatenate([c_sample, c_prompt, jnp.zeros((8 - batch, d), F32)], axis=0)
    prompt_mod_blk = nb // 8
    mod = ada_mod(c_all, w_ada, b_ada)

    alog_row = _head_row(a_log, 0)
    dtb_row = _head_row(dt_bias, 0)
    w_o_bf = w_o.astype(BF16)
    wr_pad = jnp.zeros((depth, d, LANES), F32).at[:, :, :N_EXPERTS].set(w_router)
    br_pad = jnp.zeros((depth, 1, LANES), F32).at[:, 0, :N_EXPERTS].set(b_router)
    sp_t = jnp.swapaxes(state_pool, 1, 2)
    ssc_t = jnp.swapaxes(state_shortconv, 1, 2)
    sq_t = jnp.swapaxes(state_qkv_conv, 1, 2)

    nb_max = (total * TOP_K + N_EXPERTS * (MOE_BM - 1)) // MOE_BM
    npad = nb_max * MOE_BM
    dff = w_down.shape[2]
    tn_gu, tn_dn = 512, 1024
    w_in_bf = w_in.astype(BF16)

    pools_p, pools_s, scs_p, scs_s, qkvs_p, qkvs_s, deltas_p = ([] for _ in range(7))
    delta_s_all = None
    for l in range(depth):
        proj_p, ab_p = in_proj(x_all, mod, norm1, w_in_bf, l, row0=0, rows=n_prompt, tm=tm_in, tn=1024,
                               per_row=False, rows_per_batch=seq, mod_row_blk=prompt_mod_blk)
        proj_s, ab_s = in_proj(x_all, mod, norm1, w_in_bf, l, row0=n_prompt, rows=nb, tm=nb, tn=1024,
                               per_row=True, rows_per_batch=nb, mod_row_blk=0)
        ya_p = pool_prompt(proj_p, pool_w, pool_scale, l, batch=batch, seq=seq)
        yb_p, sc_new_p = sc_prompt(proj_p, sc_conv_w, l, batch=batch, seq=seq, width=sc_width, col0=col_sc)
        yc_p, s_new_p = gdn_prompt(proj_p, ab_p, qkv_conv_w, alog_row, dtb_row, onorm_w, l,
                                   batch=batch, seq=seq, col0=col_gdn)
        ya_s, yb_s, cx_s, qn, kn, vv, eg, beta = mix_sample(
            proj_s, ab_s, sp_t[l], ssc_t[l], sq_t[l], pool_w, pool_scale, sc_conv_w, qkv_conv_w,
            alog_row, dtb_row, l)
        yc_s, delta_s_all = delta_sample(qn, kn, vv, eg, beta, proj_s, onorm_w, state_delta, l,
                                         zcol_blk=(col_gdn + 3 * gdn_width) // gdn_width, prev=delta_s_all)

        p3 = proj_p.reshape(batch, seq, -1)
        pools_p.append(p3[:, seq - 15:, :pool_width])
        scs_p.append(sc_new_p)
        qkvs_p.append(p3[:, seq - 3:, col_gdn:col_gdn + 3 * gdn_width])
        deltas_p.append(s_new_p)
        pools_s.append(jnp.concatenate([state_pool[l][:, 1:], proj_s[:, None, :pool_width]], axis=1))
        scs_s.append(jnp.concatenate([state_shortconv[l][:, 1:], cx_s[:, None, :]], axis=1))
        qkvs_s.append(jnp.concatenate(
            [state_qkv_conv[l][:, 1:], proj_s[:, None, col_gdn:col_gdn + 3 * gdn_width]], axis=1))

        shared = o_proj(ya_p, yb_p, yc_p, x_all, mod, norm2, w_o_bf, wr_pad, br_pad, l,
                        row0=0, rows=n_prompt, tm=tm_o, per_row=False, rows_per_batch=seq,
                        mod_row_blk=prompt_mod_blk, total_rows=total)
        x1, xu, topi, topp = o_proj(ya_s, yb_s, yc_s, x_all, mod, norm2, w_o_bf, wr_pad, br_pad, l,
                                    row0=n_prompt, rows=nb, tm=nb, per_row=True, rows_per_batch=nb,
                                    mod_row_blk=0, total_rows=total, prev=shared)

        rank, cnt = route_rank(topi, tb=5 * LANES if total % (5 * LANES) == 0 else LANES)
        counts = cnt[0, :N_EXPERTS]
        (work_gu, work_dn), row_off, tail_blk = make_work_lists(counts, (dff // tn_gu, d // tn_dn), nb_max)
        ti4 = topi[:, :TOP_K]
        pos_flat = (_table_lookup(row_off, ti4) + rank[:, :TOP_K]).reshape(-1).astype(I32)
        xs = dispatch(pos_flat, xu, zero_tail_blocks(tail_blk, npad, d // 2))
        act = moe_gate_up(work_gu, xs, w_gate_up, b_gate_up, l, tn=tn_gu)
        y_sorted = moe_down(work_dn, act, w_down, b_down, l, tn=tn_dn)
        x_all = combine(pos_flat, y_sorted, topp, x1, mod, l, n_prompt=n_prompt,
                        rows_per_batch=seq, prompt_mod_blk=prompt_mod_blk)

    y_p, y_s = final_norm_split(x_all, final_norm, n_prompt=n_prompt, tb=nb)
    return (y_p.reshape(batch, seq, d), y_s.reshape(nb, 1, d),
            jnp.stack(pools_p), jnp.stack(pools_s), jnp.stack(scs_p), jnp.stack(scs_s),
            jnp.stack(qkvs_p), jnp.stack(qkvs_s), jnp.stack(deltas_p), delta_s_all)
```

```python
import functools

import jax
import jax.numpy as jnp
from jax import lax
from jax.experimental import pallas as pl
from jax.experimental.pallas import tpu as pltpu

F32 = jnp.float32
BF16 = jnp.bfloat16
I32 = jnp.int32
U32 = jnp.uint32

EPS = 1e-6
POOL_WINDOWS = (2, 4, 8, 16)
POOL_GROUP = 128
GDN_HEADS = 8
GDN_DIM = 128
GDN_CHUNK = 64
GDN_STEP_CHUNKS = 4
N_EXPERTS = 32
TOP_K = 4
SWIGLU_LIMIT = 7.0
SWIGLU_ALPHA = 1.702
LANES = 128
MOE_BM = 512
MOE_HALF = MOE_BM // 2
VMEM_LIMIT = 56 * 1024 * 1024


def _cp(sem):
    return pltpu.CompilerParams(dimension_semantics=sem, vmem_limit_bytes=VMEM_LIMIT)


def _silu(x):
    return x * jax.nn.sigmoid(x)


def _softplus(x):
    return jnp.maximum(x, 0.0) + jnp.log1p(jnp.exp(-jnp.abs(x)))


def _bdot(a, b):
    return jnp.dot(a.astype(BF16), b.astype(BF16), preferred_element_type=F32)


def _shift_rows(x, s, row):
    return jnp.where(row >= s, pltpu.roll(x, s, 0), 0.0)


def _ada_kernel(c_ref, w_ref, b_ref, o_ref):
    o_ref[...] = _bdot(_silu(c_ref[...]), w_ref[...]) + b_ref[...]


def ada_mod(c_all, w_ada, b_ada, tn=1024):
    depth, d, n = w_ada.shape
    rows = c_all.shape[0]
    return pl.pallas_call(
        _ada_kernel,
        out_shape=jax.ShapeDtypeStruct((depth, rows, n), F32),
        grid=(depth, n // tn),
        in_specs=[
            pl.BlockSpec((rows, d), lambda l, j: (0, 0)),
            pl.BlockSpec((None, d, tn), lambda l, j: (l, 0, j)),
            pl.BlockSpec((None, 1, tn), lambda l, j: (l, 0, j)),
        ],
        out_specs=pl.BlockSpec((None, rows, tn), lambda l, j: (l, 0, j)),
        compiler_params=_cp(("arbitrary", "arbitrary")),
        name="ada_mod",
    )(c_all, w_ada, b_ada.reshape(depth, 1, n))


def _mod_rows(ref, per_row, brow):
    if per_row:
        return ref[...]
    return ref[pl.ds(brow, 1), :]


def _inproj_kernel(x_ref, sh_ref, sc_ref, n_ref, w_ref, wab_ref, o_ref, oab_ref, h_scr,
                   *, per_row, blocks_per_batch, n_ab):
    i = pl.program_id(0)
    j = pl.program_id(1)

    @pl.when(j == 0)
    def _():
        x = x_ref[...]
        y = x * lax.rsqrt(jnp.mean(x * x, -1, keepdims=True) + EPS) * n_ref[...]
        brow = i // blocks_per_batch
        h = y * (1.0 + _mod_rows(sc_ref, per_row, brow)) + _mod_rows(sh_ref, per_row, brow)
        hb = h.astype(BF16)
        h_scr[...] = hb
        col = lax.broadcasted_iota(I32, wab_ref.shape, 1)
        wab = jnp.where(col < n_ab, wab_ref[...], 0.0)
        oab_ref[...] = _bdot(hb, wab)

    o_ref[...] = jnp.dot(h_scr[...], w_ref[...], preferred_element_type=F32)


def in_proj(x_all, mod, norm_w, w_in, layer, *, row0, rows, tm, tn, per_row, rows_per_batch, mod_row_blk):
    depth, d, n_cols = w_in.shape
    n_main = (n_cols // LANES) * LANES
    n_ab = n_cols - n_main
    mrows = tm if per_row else 8
    mod_spec = lambda chunk: pl.BlockSpec((None, mrows, d), lambda i, j: (layer, mod_row_blk, chunk))
    kern = functools.partial(_inproj_kernel, per_row=per_row,
                             blocks_per_batch=max(rows_per_batch // tm, 1), n_ab=n_ab)
    return pl.pallas_call(
        kern,
        out_shape=(jax.ShapeDtypeStruct((rows, n_main), F32), jax.ShapeDtypeStruct((rows, LANES), F32)),
        grid=(rows // tm, n_main // tn),
        in_specs=[
            pl.BlockSpec((tm, d), lambda i, j: (row0 // tm + i, 0)),
            mod_spec(0), mod_spec(1),
            pl.BlockSpec((None, 1, d), lambda i, j: (layer, 0, 0)),
            pl.BlockSpec((None, d, tn), lambda i, j: (layer, 0, j)),
            pl.BlockSpec((None, d, LANES), lambda i, j: (layer, 0, n_main // LANES)),
        ],
        out_specs=(pl.BlockSpec((tm, tn), lambda i, j: (i, j)),
                   pl.BlockSpec((tm, LANES), lambda i, j: (i, 0))),
        scratch_shapes=[pltpu.VMEM((tm, d), BF16)],
        compiler_params=_cp(("arbitrary", "arbitrary")),
        name="in_proj",
    )(x_all, mod, mod, norm_w.reshape(depth, 1, d), w_in, w_in)


def _pool_p_kernel(u_ref, w_ref, s_ref, o_ref):
    g = pl.program_id(1)
    u = u_ref[...]
    row = lax.broadcasted_iota(I32, u.shape, 0)
    w2 = u + _shift_rows(u, 1, row)
    w4 = w2 + _shift_rows(w2, 2, row)
    w8 = w4 + _shift_rows(w4, 4, row)
    w16 = w8 + _shift_rows(w8, 8, row)
    wsum = jnp.where(g == 0, w2, jnp.where(g == 1, w4, jnp.where(g == 2, w8, w16)))
    win = jnp.left_shift(2, g)
    cnt = jnp.minimum(row + 1, win).astype(F32)
    d = wsum / cnt - u
    o_ref[...] = (_bdot(d, w_ref[...]) * s_ref[...]).astype(o_ref.dtype)


def pool_prompt(proj, pool_w, pool_scale, layer, *, batch, seq):
    depth = pool_w.shape[0]
    ng = len(POOL_WINDOWS)
    return pl.pallas_call(
        _pool_p_kernel,
        out_shape=jax.ShapeDtypeStruct((batch * seq, ng * POOL_GROUP), BF16),
        grid=(batch, ng),
        in_specs=[
            pl.BlockSpec((seq, POOL_GROUP), lambda b, g: (b, g)),
            pl.BlockSpec((None, None, POOL_GROUP, POOL_GROUP), lambda b, g: (layer, g, 0, 0)),
            pl.BlockSpec((None, 1, POOL_GROUP), lambda b, g: (layer, 0, g)),
        ],
        out_specs=pl.BlockSpec((seq, POOL_GROUP), lambda b, g: (b, g)),
        compiler_params=_cp(("arbitrary", "arbitrary")),
        name="pool_prompt",
    )(proj, pool_w, pool_scale.reshape(depth, 1, ng * POOL_GROUP))


def _sc_p_kernel(x_ref, b_ref, c_ref, w_ref, o_ref, ns_ref):
    cx = c_ref[...] * x_ref[...]
    row = lax.broadcasted_iota(I32, cx.shape, 0)
    w = w_ref[...]
    y = _shift_rows(cx, 2, row) * w[0:1] + _shift_rows(cx, 1, row) * w[1:2] + cx * w[2:3]
    o_ref[...] = (b_ref[...] * y).astype(o_ref.dtype)
    n = cx.shape[0]
    ns_ref[...] = cx[n - 2:n, :]


def sc_prompt(proj, sc_conv_w, layer, *, batch, seq, width, col0, tc=256):
    nct = width // tc
    cb = col0 // tc
    return pl.pallas_call(
        _sc_p_kernel,
        out_shape=(jax.ShapeDtypeStruct((batch * seq, width), BF16),
                   jax.ShapeDtypeStruct((batch, 2, width), F32)),
        grid=(batch, nct),
        in_specs=[
            pl.BlockSpec((seq, tc), lambda b, c: (b, cb + c)),
            pl.BlockSpec((seq, tc), lambda b, c: (b, cb + nct + c)),
            pl.BlockSpec((seq, tc), lambda b, c: (b, cb + 2 * nct + c)),
            pl.BlockSpec((None, 3, tc), lambda b, c: (layer, 0, c)),
        ],
        out_specs=(pl.BlockSpec((seq, tc), lambda b, c: (b, c)),
                   pl.BlockSpec((None, 2, tc), lambda b, c: (b, 0, c))),
        compiler_params=_cp(("arbitrary", "arbitrary")),
        name="sc_prompt",
    )(proj, proj, proj, sc_conv_w)


def _inv_unit_lower_minus_eye(lms):
    c = lms[0].shape[0]
    ps = [-lm for lm in lms]
    rs = list(ps)
    steps = max(c.bit_length() - 2, 0)
    for _ in range(steps):
        ps = [_bdot(p, p) for p in ps]
        rs = [r + p + _bdot(r, p) for r, p in zip(rs, ps)]
    return rs


def _gdn_p_kernel(qc_ref, kc_ref, vc_ref, qp_ref, kp_ref, vp_ref, z_ref, ab_ref,
                  cwq_ref, cwk_ref, cwv_ref, alog_ref, dtb_ref, on_ref,
                  o_ref, sout_ref, s_scr):
    n = pl.program_id(1)
    rows = qc_ref.shape[0]
    c = GDN_CHUNK
    hd = GDN_DIM

    @pl.when(n == 0)
    def _():
        s_scr[...] = jnp.zeros(s_scr.shape, F32)

    has_prev = n > 0

    def conv_silu(cur_ref, prev_ref, w_ref):
        prev = jnp.where(has_prev, prev_ref[...], 0.0)
        xx = jnp.concatenate([prev, cur_ref[...]], axis=0)
        w = w_ref[...]
        y = (xx[5:5 + rows] * w[0:1] + xx[6:6 + rows] * w[1:2]
             + xx[7:7 + rows] * w[2:3] + xx[8:8 + rows] * w[3:4])
        return _silu(y)

    q = conv_silu(qc_ref, qp_ref, cwq_ref)
    k = conv_silu(kc_ref, kp_ref, cwk_ref)
    v = conv_silu(vc_ref, vp_ref, cwv_ref)
    z = z_ref[...]

    ab = ab_ref[...]
    g_all = -jnp.exp(alog_ref[...]) * _softplus(ab + dtb_ref[...])
    beta_all = jax.nn.sigmoid(ab)
    row_in_chunk = lax.broadcasted_iota(I32, g_all.shape, 0) % c
    gc = g_all
    s = 1
    while s < c:
        gc = gc + _shift_rows(gc, s, row_in_chunk)
        s *= 2
    eg_all = jnp.exp(gc)

    ri = lax.broadcasted_iota(I32, (c, c), 0)
    ci = lax.broadcasted_iota(I32, (c, c), 1)
    incl = ri >= ci
    strict = ri > ci
    on = on_ref[...]

    heads = range(GDN_HEADS)
    chunks = range(rows // c)
    nt = (((1,), (1,)), ((), ()))
    tn_dims = (((0,), (0,)), ((), ()))
    qns, kns, rhss, lms, aintras = [], [], [], [], []
    for ch in chunks:
        rs_ = slice(ch * c, (ch + 1) * c)
        gct = gc[rs_].T
        for h in heads:
            sl = slice(h * hd, (h + 1) * hd)
            qh, kh, vh = q[rs_, sl], k[rs_, sl], v[rs_, sl]
            qn = qh * lax.rsqrt(jnp.sum(qh * qh, -1, keepdims=True) + EPS) * (hd ** -0.5)
            kn = kh * lax.rsqrt(jnp.sum(kh * kh, -1, keepdims=True) + EPS)
            gcol = gc[rs_, h:h + 1]
            grow = gct[h:h + 1, :]
            bcol = beta_all[rs_, GDN_HEADS + h:GDN_HEADS + h + 1]
            decay = jnp.where(incl, jnp.exp(jnp.minimum(gcol - grow, 0.0)), 0.0)
            kb = kn * bcol
            knb = kn.astype(BF16)
            kk = lax.dot_general(kb.astype(BF16), knb, nt, preferred_element_type=F32)
            qk = lax.dot_general(qn.astype(BF16), knb, nt, preferred_element_type=F32)
            lms.append(jnp.where(strict, kk * decay, 0.0))
            aintras.append(jnp.where(incl, qk * decay, 0.0))
            rhss.append(jnp.concatenate([vh * bcol, kb * eg_all[rs_, h:h + 1]], axis=-1))
            qns.append(qn)
            kns.append(kn)
    rs = _inv_unit_lower_minus_eye(lms)
    sols = [rhs + _bdot(r, rhs) for r, rhs in zip(rs, rhss)]
    for ch in chunks:
        rs_ = slice(ch * c, (ch + 1) * c)
        for h in heads:
            i = ch * GDN_HEADS + h
            sl = slice(h * hd, (h + 1) * hd)
            gcol = gc[rs_, h:h + 1]
            u_val, k_cum = sols[i][:, :hd], sols[i][:, hd:]
            s_old = s_scr[h]
            v_new = u_val - _bdot(k_cum, s_old)
            o = _bdot(qns[i] * eg_all[rs_, h:h + 1], s_old) + _bdot(aintras[i], v_new)
            glast = gc[(ch + 1) * c - 1:(ch + 1) * c, h:h + 1]
            kd = kns[i] * jnp.exp(glast - gcol)
            s_scr[h] = s_old * jnp.exp(glast) + lax.dot_general(
                kd.astype(BF16), v_new.astype(BF16), tn_dims, preferred_element_type=F32)
            o = o * lax.rsqrt(jnp.mean(o * o, -1, keepdims=True) + EPS) * on
            o_ref[rs_, sl] = (o * _silu(z[rs_, sl])).astype(o_ref.dtype)

    @pl.when(n == pl.num_programs(1) - 1)
    def _():
        sout_ref[...] = s_scr[...]


def _head_row(v, offset):
    depth, h = v.shape
    return jnp.zeros((depth, 1, LANES), F32).at[:, 0, offset:offset + h].set(v.astype(F32))


def gdn_prompt(proj, ab, qkv_conv_w, alog_row, dtb_row, onorm_w, layer, *, batch, seq, col0):
    depth = qkv_conv_w.shape[0]
    wdt = GDN_HEADS * GDN_DIM
    c = GDN_CHUNK * GDN_STEP_CHUNKS
    nchunk = seq // c
    cb = col0 // wdt
    cur = lambda t: pl.BlockSpec((c, wdt), lambda b, n: (b * nchunk + n, cb + t))
    prev = lambda t: pl.BlockSpec(
        (8, wdt), lambda b, n: (jnp.maximum((b * nchunk + n) * (c // 8) - 1, 0), cb + t))
    cw = lambda t: pl.BlockSpec((None, 4, wdt), lambda b, n: (layer, 0, t))
    row = pl.BlockSpec((None, 1, LANES), lambda b, n: (layer, 0, 0))
    return pl.pallas_call(
        _gdn_p_kernel,
        out_shape=(jax.ShapeDtypeStruct((batch * seq, wdt), BF16),
                   jax.ShapeDtypeStruct((batch, GDN_HEADS, GDN_DIM, GDN_DIM), F32)),
        grid=(batch, nchunk),
        in_specs=[cur(0), cur(1), cur(2), prev(0), prev(1), prev(2), cur(3),
                  pl.BlockSpec((c, LANES), lambda b, n: (b * nchunk + n, 0)),
                  cw(0), cw(1), cw(2), row, row, row],
        out_specs=(pl.BlockSpec((c, wdt), lambda b, n: (b * nchunk + n, 0)),
                   pl.BlockSpec((None, GDN_HEADS, GDN_DIM, GDN_DIM), lambda b, n: (b, 0, 0, 0))),
        scratch_shapes=[pltpu.VMEM((GDN_HEADS, GDN_DIM, GDN_DIM), F32)],
        compiler_params=_cp(("arbitrary", "arbitrary")),
        name="gdn_prompt",
    )(proj, proj, proj, proj, proj, proj, proj, ab,
      qkv_conv_w, qkv_conv_w, qkv_conv_w, alog_row, dtb_row,
      onorm_w.reshape(depth, 1, GDN_DIM))


def _mix_s_kernel(proj_ref, ab_ref, sp_ref, ssc_ref, sq_ref, pw_ref, ps_ref, scw_ref, cw_ref,
                  alog_ref, dtb_ref,
                  ya_ref, yb_ref, cx_ref, q_ref, k_ref, v_ref, eg_ref, beta_ref,
                  *, pool_w, sc_w, gdn_w):
    for g, win in enumerate(POOL_WINDOWS):
        sl = slice(g * POOL_GROUP, (g + 1) * POOL_GROUP)
        u = proj_ref[:, sl]
        acc = u
        nbuf = sp_ref.shape[0]
        for r in range(nbuf - (win - 1), nbuf):
            acc = acc + sp_ref[r, :, sl]
        d = acc / float(win) - u
        ya_ref[:, sl] = (_bdot(d, pw_ref[g]) * ps_ref[:, sl]).astype(ya_ref.dtype)
    c0 = pool_w
    xs = proj_ref[:, c0:c0 + sc_w]
    bg = proj_ref[:, c0 + sc_w:c0 + 2 * sc_w]
    cg = proj_ref[:, c0 + 2 * sc_w:c0 + 3 * sc_w]
    cx = cg * xs
    w = scw_ref[...]
    y = ssc_ref[0] * w[0:1] + ssc_ref[1] * w[1:2] + cx * w[2:3]
    yb_ref[...] = (bg * y).astype(yb_ref.dtype)
    cx_ref[...] = cx
    c1 = c0 + 3 * sc_w
    hd = GDN_DIM
    outs = (q_ref, k_ref, v_ref)
    for t in range(3):
        new = proj_ref[:, c1 + t * gdn_w:c1 + (t + 1) * gdn_w]
        ws = slice(t * gdn_w, (t + 1) * gdn_w)
        y = (sq_ref[0, :, ws] * cw_ref[0:1, ws] + sq_ref[1, :, ws] * cw_ref[1:2, ws]
             + sq_ref[2, :, ws] * cw_ref[2:3, ws] + new * cw_ref[3:4, ws])
        y = _silu(y)
        for h in range(GDN_HEADS):
            sl = slice(h * hd, (h + 1) * hd)
            yh = y[:, sl]
            if t == 0:
                yh = yh * lax.rsqrt(jnp.sum(yh * yh, -1, keepdims=True) + EPS) * (hd ** -0.5)
            elif t == 1:
                yh = yh * lax.rsqrt(jnp.sum(yh * yh, -1, keepdims=True) + EPS)
            outs[t][:, sl] = yh
    ab = ab_ref[...]
    eg_ref[...] = jnp.exp(-jnp.exp(alog_ref[...]) * _softplus(ab + dtb_ref[...]))
    beta_ref[...] = jax.nn.sigmoid(ab)


def mix_sample(proj, ab, sp_t, ssc_t, sq_t, pool_w, pool_scale, sc_conv_w, qkv_conv_w,
               alog_row, dtb_row, layer):
    nb = proj.shape[0]
    depth = pool_w.shape[0]
    pw = pool_scale.shape[1]
    scw = sc_conv_w.shape[2]
    gw = qkv_conv_w.shape[2] // 3
    full = lambda a: pl.BlockSpec(a.shape, lambda i: (0,) * a.ndim)
    lay = lambda a: pl.BlockSpec((None,) + a.shape[1:], lambda i: (layer,) + (0,) * (a.ndim - 1))
    ps3 = pool_scale.reshape(depth, 1, pw)
    kern = functools.partial(_mix_s_kernel, pool_w=pw, sc_w=scw, gdn_w=gw)
    shp = lambda w, dt: jax.ShapeDtypeStruct((nb, w), dt)
    outs = (shp(pw, BF16), shp(scw, BF16), shp(scw, F32), shp(gw, F32), shp(gw, F32), shp(gw, F32),
            shp(LANES, F32), shp(LANES, F32))
    return pl.pallas_call(
        kern,
        out_shape=outs,
        grid=(1,),
        in_specs=[full(proj), full(ab), full(sp_t), full(ssc_t), full(sq_t),
                  lay(pool_w), lay(ps3), lay(sc_conv_w), lay(qkv_conv_w), lay(alog_row), lay(dtb_row)],
        out_specs=tuple(pl.BlockSpec(o.shape, lambda i: (0, 0)) for o in outs),
        compiler_params=_cp(("arbitrary",)),
        name="mix_sample",
    )(proj, ab, sp_t, ssc_t, sq_t, pool_w, ps3, sc_conv_w, qkv_conv_w, alog_row, dtb_row)


def _delta_s_kernel(q_ref, k_ref, v_ref, eg_ref, beta_ref, z_ref, on_ref, s_ref, *rest, bb):
    o_ref, so_ref = rest[-2:]
    i = pl.program_id(0)
    hd = GDN_DIM
    on = on_ref[...]
    tn_dims = (((0,), (0,)), ((), ()))
    heads = range(GDN_HEADS)

    def body(r, carry):
        t = i * bb + r
        qrow = q_ref[pl.ds(t, 1), :]
        krow = k_ref[pl.ds(t, 1), :]
        vrow = v_ref[pl.ds(t, 1), :]
        egrow = eg_ref[pl.ds(t, 1), :]
        brow = beta_ref[pl.ds(t, 1), :]
        zrow = z_ref[pl.ds(t, 1), :]
        ress = []
        for h in heads:
            sl = slice(h * hd, (h + 1) * hd)
            kq = jnp.concatenate([krow[:, sl], qrow[:, sl], jnp.zeros((6, hd), F32)], axis=0)
            ress.append(_bdot(kq, s_ref[r, h]))
        upds = []
        for h in heads:
            sl = slice(h * hd, (h + 1) * hd)
            qh, kh, vh = qrow[:, sl], krow[:, sl], vrow[:, sl]
            eg = egrow[:, h:h + 1]
            bt = brow[:, GDN_HEADS + h:GDN_HEADS + h + 1]
            delta = (vh - eg * ress[h][0:1]) * bt
            qk = jnp.sum(qh * kh, -1, keepdims=True)
            o = eg * ress[h][1:2] + qk * delta
            o = o * lax.rsqrt(jnp.mean(o * o, -1, keepdims=True) + EPS) * on
            o_ref[r, :, sl] = o * _silu(zrow[:, sl])
            khi = kh.astype(BF16).astype(F32)
            dhi = delta.astype(BF16).astype(F32)
            zpad = jnp.zeros((13, hd), F32)
            lhs = jnp.concatenate([khi, kh - khi, khi, zpad], axis=0).astype(BF16)
            rhs = jnp.concatenate([dhi, dhi, delta - dhi, zpad], axis=0).astype(BF16)
            upds.append(lax.dot_general(lhs, rhs, tn_dims, preferred_element_type=F32))
        for h in heads:
            so_ref[r, h] = s_ref[r, h] * egrow[:, h:h + 1] + upds[h]
        return carry

    lax.fori_loop(0, bb, body, 0)


def delta_sample(qn, kn, vv, eg, beta, proj, onorm_w, state_delta, layer, *, zcol_blk, prev=None, bb=8):
    nb, gw = qn.shape
    depth = onorm_w.shape[0]
    full = lambda a: pl.BlockSpec(a.shape, lambda i: (0, 0))
    sblk = (None, bb, GDN_HEADS, GDN_DIM, GDN_DIM)
    in_specs = [full(qn), full(kn), full(vv), full(eg), full(beta),
                pl.BlockSpec((nb, gw), lambda i: (0, zcol_blk)),
                pl.BlockSpec((None, 1, GDN_DIM), lambda i: (layer, 0, 0)),
                pl.BlockSpec(sblk, lambda i: (layer, i, 0, 0, 0))]
    args = [qn, kn, vv, eg, beta, proj, onorm_w.reshape(depth, 1, GDN_DIM), state_delta]
    aliases = {}
    if prev is not None:
        in_specs.append(pl.BlockSpec(memory_space=pl.ANY))
        args.append(prev)
        aliases = {len(args) - 1: 1}
    o, s_new = pl.pallas_call(
        functools.partial(_delta_s_kernel, bb=bb),
        out_shape=(jax.ShapeDtypeStruct((nb, 1, gw), F32),
                   jax.ShapeDtypeStruct(state_delta.shape, F32)),
        grid=(nb // bb,),
        in_specs=in_specs,
        out_specs=(pl.BlockSpec((bb, 1, gw), lambda i: (i, 0, 0)),
                   pl.BlockSpec(sblk, lambda i: (layer, i, 0, 0, 0))),
        input_output_aliases=aliases,
        compiler_params=_cp(("arbitrary",)),
        name="delta_sample",
    )(*args)
    return o.reshape(nb, gw), s_new


def _pack_halves(h):
    half = h.shape[1] // 2
    lo = pltpu.bitcast(h[:, :half].astype(BF16).astype(F32), U32)
    hi = pltpu.bitcast(h[:, half:].astype(BF16).astype(F32), U32)
    return (lo >> 16) | (hi & jnp.uint32(0xFFFF0000))


def _unpack_halves(xu):
    lo = pltpu.bitcast(xu << 16, F32).astype(BF16)
    hi = pltpu.bitcast(xu & jnp.uint32(0xFFFF0000), F32).astype(BF16)
    return lo, hi


def _oproj_kernel(*refs, per_row, blocks_per_batch, n_alias):
    (ya_ref, yb_ref, yc_ref, x_ref, g1_ref, sh_ref, sc_ref, n_ref,
     woa_ref, wob_ref, woc_ref, wr_ref, br_ref) = refs[:13]
    x1_ref, xu_ref, ti_ref, tp_ref = refs[13 + n_alias:13 + n_alias + 4]
    i = pl.program_id(0)
    brow = i // blocks_per_batch
    acc = (jnp.dot(ya_ref[...].astype(BF16), woa_ref[...], preferred_element_type=F32)
           + jnp.dot(yb_ref[...].astype(BF16), wob_ref[...], preferred_element_type=F32)
           + jnp.dot(yc_ref[...].astype(BF16), woc_ref[...], preferred_element_type=F32))
    x1 = x_ref[...] + _mod_rows(g1_ref, per_row, brow) * acc
    x1_ref[...] = x1
    y = x1 * lax.rsqrt(jnp.mean(x1 * x1, -1, keepdims=True) + EPS) * n_ref[...]
    h = y * (1.0 + _mod_rows(sc_ref, per_row, brow)) + _mod_rows(sh_ref, per_row, brow)
    xu_ref[...] = _pack_halves(h)
    hhi = h.astype(BF16)
    hlo = (h - hhi.astype(F32)).astype(BF16)
    wr = wr_ref[...]
    whi = wr.astype(BF16)
    wlo = (wr - whi.astype(F32)).astype(BF16)
    logits = (jnp.dot(hhi, whi, preferred_element_type=F32)
              + jnp.dot(hlo, whi, preferred_element_type=F32)
              + jnp.dot(hhi, wlo, preferred_element_type=F32)) + br_ref[...]
    lane = lax.broadcasted_iota(I32, logits.shape, 1)
    neg = jnp.float32(-jnp.inf)
    cur = jnp.where(lane < N_EXPERTS, logits, neg)
    ti = jnp.zeros(logits.shape, I32)
    tv = jnp.full(logits.shape, neg, F32)
    for kk in range(TOP_K):
        m = jnp.max(cur, -1, keepdims=True)
        idx = jnp.min(jnp.where(cur == m, lane, LANES), -1, keepdims=True)
        ti = jnp.where(lane == kk, idx, ti)
        tv = jnp.where(lane == kk, m, tv)
        cur = jnp.where(lane == idx, neg, cur)
    e = jnp.exp(tv - jnp.max(tv, -1, keepdims=True))
    ti_ref[...] = ti
    tp_ref[...] = e / jnp.sum(e, -1, keepdims=True)


def o_proj(ya, yb, yc, x_all, mod, norm_w, w_o_bf, w_router_pad, b_router_pad, layer, *,
           row0, rows, tm, per_row, rows_per_batch, mod_row_blk, total_rows, prev=None):
    depth, d, _ = w_o_bf.shape
    wa, wb, wc = ya.shape[1], yb.shape[1], yc.shape[1]
    mrows = tm if per_row else 8
    rb = row0 // tm
    mod_spec = lambda chunk: pl.BlockSpec((None, mrows, d), lambda i: (layer, mod_row_blk, chunk))
    n_alias = 0 if prev is None else 4
    kern = functools.partial(_oproj_kernel, per_row=per_row,
                             blocks_per_batch=max(rows_per_batch // tm, 1), n_alias=n_alias)
    in_specs = [
        pl.BlockSpec((tm, wa), lambda i: (i, 0)),
        pl.BlockSpec((tm, wb), lambda i: (i, 0)),
        pl.BlockSpec((tm, wc), lambda i: (i, 0)),
        pl.BlockSpec((tm, d), lambda i: (rb + i, 0)),
        mod_spec(2), mod_spec(3), mod_spec(4),
        pl.BlockSpec((None, 1, d), lambda i: (layer, 0, 0)),
        pl.BlockSpec((None, wa, d), lambda i: (layer, 0, 0)),
        pl.BlockSpec((None, wb, d), lambda i: (layer, wa // wb, 0)),
        pl.BlockSpec((None, wc, d), lambda i: (layer, (wa + wb) // wc, 0)),
        pl.BlockSpec((None, d, LANES), lambda i: (layer, 0, 0)),
        pl.BlockSpec((None, 1, LANES), lambda i: (layer, 0, 0)),
    ]
    args = [ya, yb, yc, x_all, mod, mod, mod, norm_w.reshape(depth, 1, d),
            w_o_bf, w_o_bf, w_o_bf, w_router_pad, b_router_pad]
    aliases = {}
    if prev is not None:
        in_specs += [pl.BlockSpec(memory_space=pl.ANY)] * 4
        args += list(prev)
        aliases = {13 + t: t for t in range(4)}
    out_shape = (jax.ShapeDtypeStruct((total_rows, d), F32),
                 jax.ShapeDtypeStruct((total_rows, d // 2), U32),
                 jax.ShapeDtypeStruct((total_rows, LANES), I32),
                 jax.ShapeDtypeStruct((total_rows, LANES), F32))
    out_specs = (pl.BlockSpec((tm, d), lambda i: (rb + i, 0)),
                 pl.BlockSpec((tm, d // 2), lambda i: (rb + i, 0)),
                 pl.BlockSpec((tm, LANES), lambda i: (rb + i, 0)),
                 pl.BlockSpec((tm, LANES), lambda i: (rb + i, 0)))
    return pl.pallas_call(
        kern, out_shape=out_shape, grid=(rows // tm,),
        in_specs=in_specs, out_specs=out_specs,
        input_output_aliases=aliases,
        compiler_params=_cp(("arbitrary",)),
        name="o_proj",
    )(*args)


def _rank_kernel(ti_ref, rank_ref, cnt_ref, run_scr):
    i = pl.program_id(0)

    @pl.when(i == 0)
    def _():
        run_scr[...] = jnp.zeros(run_scr.shape, F32)

    ti = ti_ref[...]
    tb = ti.shape[0]
    lane = lax.broadcasted_iota(I32, ti.shape, 1)
    sel = jnp.zeros(ti.shape, F32)
    for kk in range(TOP_K):
        sel = sel + jnp.where(lane == ti[:, kk:kk + 1], 1.0, 0.0)
    r = lax.broadcasted_iota(I32, (tb, tb), 0)
    q = lax.broadcasted_iota(I32, (tb, tb), 1)
    tri = jnp.where(r > q, 1.0, 0.0)
    before = _bdot(tri, sel) + run_scr[0:1, :]
    rank = jnp.zeros(ti.shape, F32)
    for kk in range(TOP_K):
        rk = jnp.sum(jnp.where(lane == ti[:, kk:kk + 1], before, 0.0), -1, keepdims=True)
        rank = jnp.where(lane == kk, rk, rank)
    rank_ref[...] = rank.astype(I32)
    total = run_scr[0:1, :] + jnp.sum(sel, 0, keepdims=True)
    run_scr[...] = jnp.broadcast_to(total, run_scr.shape)
    cnt_ref[...] = jnp.broadcast_to(total, cnt_ref.shape).astype(I32)


def route_rank(topi, tb=128):
    t = topi.shape[0]
    return pl.pallas_call(
        _rank_kernel,
        out_shape=(jax.ShapeDtypeStruct((t, LANES), I32), jax.ShapeDtypeStruct((8, LANES), I32)),
        grid=(t // tb,),
        in_specs=[pl.BlockSpec((tb, LANES), lambda i: (i, 0))],
        out_specs=(pl.BlockSpec((tb, LANES), lambda i: (i, 0)),
                   pl.BlockSpec((8, LANES), lambda i: (0, 0))),
        scratch_shapes=[pltpu.VMEM((8, LANES), F32)],
        compiler_params=_cp(("arbitrary",)),
        name="route_rank",
    )(topi)


def _dispatch_kernel(pos_ref, xu_ref, xs_in_ref, xs_ref, sem, *, tb):
    del xs_in_ref
    i = pl.program_id(0)
    base = i * tb * TOP_K

    def row_copy(r, kk):
        p = pos_ref[base + r * TOP_K + kk]
        return pltpu.make_async_copy(xu_ref.at[pl.ds(r, 1)], xs_ref.at[pl.ds(p, 1)], sem)

    def wait(r, c):
        for kk in range(TOP_K):
            row_copy(r, kk).wait()
        return c

    for r in range(tb):
        for kk in range(TOP_K):
            row_copy(r, kk).start(priority=kk % 2)
    lax.fori_loop(0, tb, wait, 0, unroll=4)


def _zero_tail_kernel(blk_ref, o_ref):
    del blk_ref
    o_ref[...] = jnp.zeros(o_ref.shape, o_ref.dtype)


def zero_tail_blocks(tail_blk, npad, width):
    return pl.pallas_call(
        _zero_tail_kernel,
        out_shape=jax.ShapeDtypeStruct((npad, width), U32),
        grid_spec=pltpu.PrefetchScalarGridSpec(
            num_scalar_prefetch=1,
            grid=(tail_blk.shape[0],),
            in_specs=[],
            out_specs=pl.BlockSpec((MOE_HALF, width), lambda e, blk: (blk[e], 0)),
        ),
        compiler_params=_cp(("arbitrary",)),
        name="moe_zero_tail",
    )(tail_blk)


def dispatch(pos_flat, xu, xs_zero, tb=128):
    t, w = xu.shape
    return pl.pallas_call(
        functools.partial(_dispatch_kernel, tb=tb),
        out_shape=jax.ShapeDtypeStruct(xs_zero.shape, xs_zero.dtype),
        grid_spec=pltpu.PrefetchScalarGridSpec(
            num_scalar_prefetch=1,
            grid=(t // tb,),
            in_specs=[pl.BlockSpec((tb, w), lambda i, pos: (i, 0)),
                      pl.BlockSpec(memory_space=pl.ANY)],
            out_specs=pl.BlockSpec(memory_space=pl.ANY),
            scratch_shapes=[pltpu.SemaphoreType.DMA(())],
        ),
        input_output_aliases={2: 0},
        compiler_params=_cp(("arbitrary",)),
        name="moe_dispatch",
    )(pos_flat, xu, xs_zero)


FLAG_VALID, FLAG_FIRST, FLAG_HALF, FLAG_SLOT, FLAG_NEXT = 1, 2, 4, 8, 16


def _expert_mm_kernel(we_ref, wj_ref, wi_ref, wfl_ref, wne_ref, wnj_ref, *refs, layer, tn, up_col0):
    gate_up = up_col0 is not None
    if gate_up:
        x_ref, w_hbm, bg_ref, bu_ref, o_ref, wf32, wbf, sem = refs
    else:
        x_ref, w_hbm, bd_ref, o_ref, wf32, wbf, sem = refs
    ntile = 2 if gate_up else 1
    w = pl.program_id(0)
    fl = wfl_ref[w]
    has = lambda bit: (fl & bit) != 0
    slot = jnp.where(has(FLAG_SLOT), 1, 0)

    def wcopy(e, j, sl, t):
        col = pl.multiple_of(j * tn + (up_col0 if t else 0), tn)
        return pltpu.make_async_copy(w_hbm.at[layer, e, :, pl.ds(col, tn)], wf32.at[sl, t], sem.at[sl, t])

    @pl.when(w == 0)
    def _():
        for t in range(ntile):
            wcopy(we_ref[0], wj_ref[0], 0, t).start()

    @pl.when(has(FLAG_FIRST))
    def _():
        for t in range(ntile):
            wcopy(we_ref[w], wj_ref[w], slot, t).wait()

        @pl.when(has(FLAG_NEXT))
        def _():
            for t in range(ntile):
                wcopy(wne_ref[w], wnj_ref[w], 1 - slot, t).start()

        for t in range(ntile):
            wbf[t] = wf32[slot, t].astype(BF16)

    def compute(m):
        if gate_up:
            lo, hi = _unpack_halves(x_ref[0:m, :])
            half = lo.shape[1]
            gate = (jnp.dot(lo, wbf[0, :half], preferred_element_type=F32)
                    + jnp.dot(hi, wbf[0, half:], preferred_element_type=F32) + bg_ref[...])
            up = (jnp.dot(lo, wbf[1, :half], preferred_element_type=F32)
                  + jnp.dot(hi, wbf[1, half:], preferred_element_type=F32) + bu_ref[...])
            gate = jnp.minimum(gate, SWIGLU_LIMIT)
            up = jnp.clip(up, -SWIGLU_LIMIT, SWIGLU_LIMIT)
            o_ref[0:m, :] = ((up + 1.0) * gate * jax.nn.sigmoid(SWIGLU_ALPHA * gate)).astype(o_ref.dtype)
        else:
            o_ref[0:m, :] = jnp.dot(x_ref[0:m, :], wbf[0], preferred_element_type=F32) + bd_ref[...]

    @pl.when(has(FLAG_VALID) & jnp.logical_not(has(FLAG_HALF)))
    def _():
        compute(MOE_BM)

    @pl.when(has(FLAG_VALID) & has(FLAG_HALF))
    def _():
        compute(MOE_HALF)


def _expert_mm(work, x, w, bias, layer, *, tn, gate_up, out_dtype):
    depth, ne, k, n_all = w.shape
    n_out = n_all // 2 if gate_up else n_all
    nj = n_out // tn
    npad = x.shape[0]
    nwork = work[0].shape[0]
    b4 = bias.reshape(depth, ne, 1, n_all)
    ntile = 2 if gate_up else 1
    x_spec = pl.BlockSpec((MOE_BM, x.shape[1]), lambda w_, we, wj, wi, *_: (wi[w_], 0))
    b_spec = lambda off: pl.BlockSpec((None, None, 1, tn), lambda w_, we, wj, *_: (layer, we[w_], 0, off + wj[w_]))
    in_specs = [x_spec, pl.BlockSpec(memory_space=pl.ANY), b_spec(0)] + ([b_spec(nj)] if gate_up else [])
    args = [x, w, b4] + ([b4] if gate_up else [])
    kern = functools.partial(_expert_mm_kernel, layer=layer, tn=tn, up_col0=n_out if gate_up else None)
    return pl.pallas_call(
        kern,
        out_shape=jax.ShapeDtypeStruct((npad, n_out), out_dtype),
        grid_spec=pltpu.PrefetchScalarGridSpec(
            num_scalar_prefetch=6,
            grid=(nwork,),
            in_specs=in_specs,
            out_specs=pl.BlockSpec((MOE_BM, tn), lambda w_, we, wj, wi, *_: (wi[w_], wj[w_])),
            scratch_shapes=[pltpu.VMEM((2, ntile, k, tn), F32), pltpu.VMEM((ntile, k, tn), BF16),
                            pltpu.SemaphoreType.DMA((2, ntile))],
        ),
        compiler_params=_cp(("arbitrary",)),
        name="moe_gate_up" if gate_up else "moe_down",
    )(*work, *args)


def moe_gate_up(work, xs, w_gate_up, b_gate_up, layer, *, tn):
    return _expert_mm(work, xs, w_gate_up, b_gate_up, layer, tn=tn, gate_up=True, out_dtype=BF16)


def moe_down(work, act, w_down, b_down, layer, *, tn):
    return _expert_mm(work, act, w_down, b_down, layer, tn=tn, gate_up=False, out_dtype=F32)


def make_work_lists(counts, njs, nb_max):
    n_half = (counts + MOE_HALF - 1) // MOE_HALF
    nb = (n_half + 1) // 2
    odd = n_half % 2
    blk_off = jnp.cumsum(nb) - nb
    total = jnp.sum(nb)
    nonempty = (nb > 0).astype(I32)
    groups_before = jnp.cumsum(nonempty) - nonempty
    ar = jnp.arange(N_EXPERTS, dtype=I32)
    later = (ar[None, :] > ar[:, None]) & (nb[None, :] > 0)
    next_e = jnp.min(jnp.where(later, ar[None, :], N_EXPERTS), axis=1)
    lists = []
    for nj in njs:
        start = nj * blk_off
        nvalid = nj * total
        w = jnp.arange(nj * nb_max, dtype=I32)
        wc = jnp.minimum(w, nvalid - 1)
        e = jnp.clip(jnp.sum((start[None, :] <= wc[:, None]).astype(I32), axis=1) - 1, 0, N_EXPERTS - 1)
        local = wc - _table_lookup(start, e)
        nbe = jnp.maximum(_table_lookup(nb, e), 1)
        j = local // nbe
        il = local % nbe
        valid = w < nvalid
        first = (il == 0) & valid
        half = (il == nbe - 1) & (_table_lookup(odd, e) == 1)
        slot = (_table_lookup(groups_before, e) * nj + j) % 2
        same_e = j + 1 < nj
        e_next = jnp.where(same_e, e, _table_lookup(next_e, e))
        j_next = jnp.where(same_e, j + 1, 0)
        has_next = e_next < N_EXPERTS
        flags = (FLAG_VALID * valid + FLAG_FIRST * first + FLAG_HALF * half + FLAG_SLOT * slot
                 + FLAG_NEXT * has_next).astype(I32)
        lists.append((e, j.astype(I32), (_table_lookup(blk_off, e) + il).astype(I32), flags,
                      jnp.minimum(e_next, N_EXPERTS - 1).astype(I32), j_next.astype(I32)))
    tail_blk = jnp.minimum(2 * blk_off + jnp.maximum(n_half, 1) - 1, 2 * nb_max - 1).astype(I32)
    return lists, blk_off * MOE_BM, tail_blk


def _table_lookup(table, idx):
    onehot = idx[..., None] == jnp.arange(table.shape[0], dtype=I32)
    return jnp.sum(jnp.where(onehot, table, 0), axis=-1).astype(I32)


def _combine_kernel(pos_ref, y_ref, tp_ref, x1_ref, g2p_ref, g2s_ref, o_ref, buf, sem,
                    *, tb, n_prompt_blocks, blocks_per_batch):
    i = pl.program_id(0)
    slot = i % 2

    def row_copy(blk, sl, r, kk):
        p = pos_ref[(blk * tb + r) * TOP_K + kk]
        return pltpu.make_async_copy(y_ref.at[pl.ds(p, 1)], buf.at[sl, kk, pl.ds(r, 1)], sem.at[sl])

    def start_block(blk, sl):
        for r in range(tb):
            for kk in range(TOP_K):
                row_copy(blk, sl, r, kk).start(priority=kk % 2)

    def wait_block(blk, sl):
        def body(r, c):
            for kk in range(TOP_K):
                row_copy(blk, sl, r, kk).wait()
            return c
        lax.fori_loop(0, tb, body, 0, unroll=4)

    @pl.when(i == 0)
    def _():
        start_block(0, 0)

    @pl.when(i + 1 < pl.num_programs(0))
    def _():
        start_block(i + 1, 1 - slot)

    wait_block(i, slot)
    tp = tp_ref[...]
    moe = buf[slot, 0] * tp[:, 0:1]
    for kk in range(1, TOP_K):
        moe = moe + buf[slot, kk] * tp[:, kk:kk + 1]
    brow = jnp.minimum(i // blocks_per_batch, 7)
    g2 = jnp.where(i >= n_prompt_blocks, g2s_ref[...], g2p_ref[pl.ds(brow, 1), :])
    o_ref[...] = x1_ref[...] + g2 * moe


def combine(pos_flat, y_sorted, topp, x1, mod, layer, *, n_prompt, rows_per_batch, prompt_mod_blk, tb=64):
    t, d = x1.shape
    npb = n_prompt // tb
    kern = functools.partial(_combine_kernel, tb=tb, n_prompt_blocks=npb,
                             blocks_per_batch=rows_per_batch // tb)
    return pl.pallas_call(
        kern,
        out_shape=jax.ShapeDtypeStruct((t, d), F32),
        grid_spec=pltpu.PrefetchScalarGridSpec(
            num_scalar_prefetch=1,
            grid=(t // tb,),
            in_specs=[
                pl.BlockSpec(memory_space=pl.ANY),
                pl.BlockSpec((tb, LANES), lambda i, pos: (i, 0)),
                pl.BlockSpec((tb, d), lambda i, pos: (i, 0)),
                pl.BlockSpec((None, 8, d), lambda i, pos: (layer, prompt_mod_blk, 5)),
                pl.BlockSpec((None, tb, d), lambda i, pos: (layer, jnp.maximum(i - npb, 0), 5)),
            ],
            out_specs=pl.BlockSpec((tb, d), lambda i, pos: (i, 0)),
            scratch_shapes=[pltpu.VMEM((2, TOP_K, tb, d), F32), pltpu.SemaphoreType.DMA((2,))],
        ),
        compiler_params=_cp(("arbitrary",)),
        name="moe_combine",
    )(pos_flat, y_sorted, topp, x1, mod, mod)


def _final_kernel(x_ref, w_ref, yp_ref, ys_ref, *, n_prompt_blocks):
    i = pl.program_id(0)
    x = x_ref[...]
    y = x * lax.rsqrt(jnp.mean(x * x, -1, keepdims=True) + EPS) * w_ref[...]

    @pl.when(i < n_prompt_blocks)
    def _():
        yp_ref[...] = y

    @pl.when(i >= n_prompt_blocks)
    def _():
        ys_ref[...] = y


def final_norm_split(x_all, w, *, n_prompt, tb):
    t, d = x_all.shape
    npb = n_prompt // tb
    return pl.pallas_call(
        functools.partial(_final_kernel, n_prompt_blocks=npb),
        out_shape=(jax.ShapeDtypeStruct((n_prompt, d), F32), jax.ShapeDtypeStruct((t - n_prompt, d), F32)),
        grid=(t // tb,),
        in_specs=[pl.BlockSpec((tb, d), lambda i: (i, 0)),
                  pl.BlockSpec((1, d), lambda i: (0, 0))],
        out_specs=(pl.BlockSpec((tb, d), lambda i: (jnp.minimum(i, npb - 1), 0)),
                   pl.BlockSpec((tb, d), lambda i: (jnp.maximum(i - npb, 0), 0))),
        compiler_params=_cp(("arbitrary",)),
        name="final_norm",
    )(x_all, w.reshape(1, d))


def kernel(x_prompt, x_sample, state_pool, state_shortconv, state_qkv_conv, state_delta,
           c_prompt, c_sample, norm1, norm2, w_ada, b_ada, w_in, pool_w, pool_scale,
           sc_conv_w, qkv_conv_w, a_log, dt_bias, onorm_w, w_o, w_router, b_router,
           w_gate_up, b_gate_up, w_down, b_down, final_norm):
    batch, seq, d = x_prompt.shape
    nb = x_sample.shape[0]
    depth = w_in.shape[0]
    n_prompt = batch * seq
    total = n_prompt + nb
    pool_width = pool_scale.shape[1]
    sc_width = sc_conv_w.shape[2]
    gdn_width = qkv_conv_w.shape[2] // 3
    col_sc = pool_width
    col_gdn = pool_width + 3 * sc_width
    tm_in = min(1024, seq)
    tm_o = min(512, seq)
    assert nb == LANES and seq % tm_in == 0 and seq % GDN_CHUNK == 0

    x_all = jnp.concatenate([x_prompt.reshape(n_prompt, d), x_sample.reshape(nb, d)], axis=0)
    c_all = jnp.concatenate([c_sample, c_prompt, jnp.zeros((8 - batch, d), F32)], axis=0)
    prompt_mod_blk = nb // 8
    mod = ada_mod(c_all, w_ada, b_ada)

    alog_row = _head_row(a_log, 0)
    dtb_row = _head_row(dt_bias, 0)
    w_o_bf = w_o.astype(BF16)
    wr_pad = jnp.zeros((depth, d, LANES), F32).at[:, :, :N_EXPERTS].set(w_router)
    br_pad = jnp.zeros((depth, 1, LANES), F32).at[:, 0, :N_EXPERTS].set(b_router)
    sp_t = jnp.swapaxes(state_pool, 1, 2)
    ssc_t = jnp.swapaxes(state_shortconv, 1, 2)
    sq_t = jnp.swapaxes(state_qkv_conv, 1, 2)

    nb_max = (total * TOP_K + N_EXPERTS * (MOE_BM - 1)) // MOE_BM
    npad = nb_max * MOE_BM
    dff = w_down.shape[2]
    tn_gu, tn_dn = 512, 1024
    w_in_bf = w_in.astype(BF16)

    pools_p, pools_s, scs_p, scs_s, qkvs_p, qkvs_s, deltas_p = ([] for _ in range(7))
    delta_s_all = None
    for l in range(depth):
        proj_p, ab_p = in_proj(x_all, mod, norm1, w_in_bf, l, row0=0, rows=n_prompt, tm=tm_in, tn=1024,
                               per_row=False, rows_per_batch=seq, mod_row_blk=prompt_mod_blk)
        proj_s, ab_s = in_proj(x_all, mod, norm1, w_in_bf, l, row0=n_prompt, rows=nb, tm=nb, tn=1024,
                               per_row=True, rows_per_batch=nb, mod_row_blk=0)
        ya_p = pool_prompt(proj_p, pool_w, pool_scale, l, batch=batch, seq=seq)
        yb_p, sc_new_p = sc_prompt(proj_p, sc_conv_w, l, batch=batch, seq=seq, width=sc_width, col0=col_sc)
        yc_p, s_new_p = gdn_prompt(proj_p, ab_p, qkv_conv_w, alog_row, dtb_row, onorm_w, l,
                                   batch=batch, seq=seq, col0=col_gdn)
        ya_s, yb_s, cx_s, qn, kn, vv, eg, beta = mix_sample(
            proj_s, ab_s, sp_t[l], ssc_t[l], sq_t[l], pool_w, pool_scale, sc_conv_w, qkv_conv_w,
            alog_row, dtb_row, l)
        yc_s, delta_s_all = delta_sample(qn, kn, vv, eg, beta, proj_s, onorm_w, state_delta, l,
                                         zcol_blk=(col_gdn + 3 * gdn_width) // gdn_width, prev=delta_s_all)

        p3 = proj_p.reshape(batch, seq, -1)
        pools_p.append(p3[:, seq - 15:, :pool_width])
        scs_p.append(sc_new_p)
        qkvs_p.append(p3[:, seq - 3:, col_gdn:col_gdn + 3 * gdn_width])
        deltas_p.append(s_new_p)
        pools_s.append(jnp.concatenate([state_pool[l][:, 1:], proj_s[:, None, :pool_width]], axis=1))
        scs_s.append(jnp.concatenate([state_shortconv[l][:, 1:], cx_s[:, None, :]], axis=1))
        qkvs_s.append(jnp.concatenate(
            [state_qkv_conv[l][:, 1:], proj_s[:, None, col_gdn:col_gdn + 3 * gdn_width]], axis=1))

        shared = o_proj(ya_p, yb_p, yc_p, x_all, mod, norm2, w_o_bf, wr_pad, br_pad, l,
                        row0=0, rows=n_prompt, tm=tm_o, per_row=False, rows_per_batch=seq,
                        mod_row_blk=prompt_mod_blk, total_rows=total)
        x1, xu, topi, topp = o_proj(ya_s, yb_s, yc_s, x_all, mod, norm2, w_o_bf, wr_pad, br_pad, l,
                                    row0=n_prompt, rows=nb, tm=nb, per_row=True, rows_per_batch=nb,
                                    mod_row_blk=0, total_rows=total, prev=shared)

        rank, cnt = route_rank(topi, tb=5 * LANES if total % (5 * LANES) == 0 else LANES)
        counts = cnt[0, :N_EXPERTS]
        (work_gu, work_dn), row_off, tail_blk = make_work_lists(counts, (dff // tn_gu, d // tn_dn), nb_max)
        ti4 = topi[:, :TOP_K]
        pos_flat = (_table_lookup(row_off, ti4) + rank[:, :TOP_K]).reshape(-1).astype(I32)
        xs = dispatch(pos_flat, xu, zero_tail_blocks(tail_blk, npad, d // 2))
        act = moe_gate_up(work_gu, xs, w_gate_up, b_gate_up, l, tn=tn_gu)
        y_sorted = moe_down(work_dn, act, w_down, b_down, l, tn=tn_dn)
        x_all = combine(pos_flat, y_sorted, topp, x1, mod, l, n_prompt=n_prompt,
                        rows_per_batch=seq, prompt_mod_blk=prompt_mod_blk)

    y_p, y_s = final_norm_split(x_all, final_norm, n_prompt=n_prompt, tb=nb)
    return (y_p.reshape(batch, seq, d), y_s.reshape(nb, 1, d),
            jnp.stack(pools_p), jnp.stack(pools_s), jnp.stack(scs_p), jnp.stack(scs_s),
            jnp.stack(qkvs_p), jnp.stack(qkvs_s), jnp.stack(deltas_p), delta_s_all)
```
